```python
import jax, jax.numpy as jnp
from jax import lax
import numpy as np

D_MODEL = 4096
BATCH = 4
SEQ = 4096
DEPTH = 1

CHUNK = 64
PLE_DIM = 256
EPS = 1e-6

GLA_HEADS = 8
GLA_DK = 128
GLA_DV = 256
GLA_GATE_RANK = 16
GLA_TAU = 16.0
MLA_HEADS = 16
MLA_Q_RANK = 1024
MLA_KV_RANK = 512
MLA_NOPE = 128
MLA_ROPE = 64
MLA_V = 128
ROPE_THETA = 10000.0
Q_BLOCK = 128
N_EXPERTS = 32
TOP_K = 4
D_EXPERT = 1024
SWIGLU_LIMIT = 7.0
SWIGLU_ALPHA = 1.702

GLA_QK_W = GLA_HEADS * GLA_DK
GLA_V_W = GLA_HEADS * GLA_DV
MLA_V_W = MLA_HEADS * MLA_V
IN_SPLITS = (GLA_QK_W, GLA_QK_W, GLA_V_W, GLA_V_W, GLA_GATE_RANK,
             MLA_Q_RANK, MLA_KV_RANK, MLA_ROPE, D_MODEL, D_MODEL)
D_IN = sum(IN_SPLITS)
IN_OFFSETS = tuple(sum(IN_SPLITS[:i + 1]) for i in range(len(IN_SPLITS) - 1))

kernel_name = 'hybrid_gla_mla_moe_ple_block'


def rmsnorm(x, g):
    xf = x.astype(jnp.float32)
    y = xf * lax.rsqrt(jnp.mean(xf * xf, axis=-1, keepdims=True) + EPS)
    return (y * g.astype(jnp.float32)).astype(x.dtype)


def apply_rope(x, cos, sin):
    xf = x.astype(jnp.float32)
    half = xf.shape[-1] // 2
    x1, x2 = xf[..., :half], xf[..., half:]
    return jnp.concatenate([x1 * cos - x2 * sin, x2 * cos + x1 * sin], axis=-1).astype(x.dtype)


def gla_mixer(q, k, v, g_out, g_low, gate_up, gate_bias, out_norm):
    B, S, _ = q.shape
    nc = S // CHUNK
    f32 = jnp.float32
    log_a = jax.nn.log_sigmoid((g_low @ gate_up + gate_bias).astype(f32)) / GLA_TAU

    def heads(t, d):
        return t.reshape(B, nc, CHUNK, GLA_HEADS, d).transpose(0, 3, 1, 2, 4).astype(f32)

    qh = heads(q, GLA_DK) * (GLA_DK ** -0.5)
    kh = heads(k, GLA_DK)
    vh = heads(v, GLA_DV)
    b = jnp.cumsum(heads(log_a, GLA_DK), axis=3)
    b_last = b[..., -1:, :]
    q_dec = qh * jnp.exp(b)
    k_dec = kh * jnp.exp(-b)
    k_to_end = kh * jnp.exp(b_last - b)
    causal = jnp.tril(jnp.ones((CHUNK, CHUNK), dtype=bool))
    scores = jnp.where(causal, jnp.einsum('bhctk,bhcsk->bhcts', q_dec, k_dec), 0.0)
    o_intra = jnp.einsum('bhcts,bhcsv->bhctv', scores, vh)
    chunk_kv = jnp.einsum('bhcsk,bhcsv->bhckv', k_to_end, vh)
    chunk_decay = jnp.exp(b_last[..., 0, :])

    def step(state, inp):
        kv_c, dec_c = inp
        return state * dec_c[..., None] + kv_c, state

    init = jnp.zeros((B, GLA_HEADS, GLA_DK, GLA_DV), f32)
    _, prev = lax.scan(step, init, (jnp.moveaxis(chunk_kv, 2, 0), jnp.moveaxis(chunk_decay, 2, 0)))
    prev = jnp.moveaxis(prev, 0, 2)
    o = o_intra + jnp.einsum('bhctk,bhckv->bhctv', q_dec, prev)
    o = o * lax.rsqrt(jnp.mean(o * o, axis=-1, keepdims=True) + EPS) * out_norm.astype(f32)
    o = o.transpose(0, 2, 3, 1, 4).reshape(B, S, GLA_V_W)
    return (o * jax.nn.silu(g_out.astype(f32))).astype(q.dtype)


def mla_mixer(c_q, c_kv, k_rope_in, q_norm, kv_norm, w_uq, w_ukv, cos, sin):
    B, S, _ = c_q.shape
    q = (rmsnorm(c_q, q_norm) @ w_uq).reshape(B, S, MLA_HEADS, MLA_NOPE + MLA_ROPE)
    q_nope = q[..., :MLA_NOPE]
    q_pe = apply_rope(q[..., MLA_NOPE:], cos[:, None, :], sin[:, None, :])
    kv = (rmsnorm(c_kv, kv_norm) @ w_ukv).reshape(B, S, MLA_HEADS, MLA_NOPE + MLA_V)
    k_nope, v = kv[..., :MLA_NOPE], kv[..., MLA_NOPE:]
    k_pe = apply_rope(k_rope_in, cos, sin)
    scale = (MLA_NOPE + MLA_ROPE) ** -0.5
    neg = jnp.finfo(jnp.float32).min
    outs = []
    for j in range(S // Q_BLOCK):
        q0, kend = j * Q_BLOCK, (j + 1) * Q_BLOCK
        s = (jnp.einsum('bqhd,bkhd->bhqk', q_nope[:, q0:kend], k_nope[:, :kend])
             + jnp.einsum('bqhr,bkr->bhqk', q_pe[:, q0:kend], k_pe[:, :kend]))
        s = s.astype(jnp.float32) * scale
        q_chunk = (q0 + jnp.arange(Q_BLOCK)) // CHUNK
        k_chunk = jnp.arange(kend) // CHUNK
        s = jnp.where(k_chunk[None, :] <= q_chunk[:, None], s, neg)
        prob = jax.nn.softmax(s, axis=-1).astype(v.dtype)
        outs.append(jnp.einsum('bhqk,bkhd->bqhd', prob, v[:, :kend]))
    return jnp.concatenate(outs, axis=1).reshape(B, S, MLA_V_W)


def moe_ffn(x, router_w, router_b, w_gu, b_gu, w_dn, b_dn):
    B, S, D = x.shape
    xt = x.reshape(B * S, D)
    logits = (xt @ router_w + router_b).astype(jnp.float32)
    top_val, top_idx = lax.top_k(logits, TOP_K)
    probs = jax.nn.softmax(top_val, axis=-1)
    combine = jnp.sum(jax.nn.one_hot(top_idx, N_EXPERTS, dtype=jnp.float32) * probs[..., None], axis=1)
    out = jnp.zeros((B * S, D), jnp.float32)
    for e in range(N_EXPERTS):
        gu = xt @ w_gu[e] + b_gu[e]
        gate = jnp.minimum(gu[:, :D_EXPERT], SWIGLU_LIMIT)
        up = jnp.clip(gu[:, D_EXPERT:], -SWIGLU_LIMIT, SWIGLU_LIMIT)
        act = (up + 1.0) * (gate * jax.nn.sigmoid(SWIGLU_ALPHA * gate))
        y = act @ w_dn[e] + b_dn[e]
        out = out + combine[:, e:e + 1] * y.astype(jnp.float32)
    return out.astype(x.dtype).reshape(B, S, D)


def hybrid_layer(h, p_i, w_in, gla_gate_up, gla_gate_bias, gla_out_norm, mla_q_norm, mla_kv_norm,
                 mla_w_uq, mla_w_ukv, w_branch_gla, w_branch_mla, w_out, norm_mix, norm_moe,
                 router_w, router_b, expert_w_gate_up, expert_b_gate_up, expert_w_down, expert_b_down,
                 norm_ple, ple_gate_w, ple_proj_w, cos, sin):
    xn = rmsnorm(h, norm_mix)
    proj = xn @ w_in
    gq, gk, gv, gg, gf, cq, ckv, kr, gate_gla, gate_mla = jnp.split(proj, IN_OFFSETS, axis=-1)
    o_gla = gla_mixer(gq, gk, gv, gg, gf, gla_gate_up, gla_gate_bias, gla_out_norm)
    o_mla = mla_mixer(cq, ckv, kr, mla_q_norm, mla_kv_norm, mla_w_uq, mla_w_ukv, cos, sin)
    merged = (jax.nn.sigmoid(gate_gla) * (o_gla @ w_branch_gla)
              + jax.nn.sigmoid(gate_mla) * (o_mla @ w_branch_mla))
    h = h + merged @ w_out
    h = h + moe_ffn(rmsnorm(h, norm_moe), router_w, router_b, expert_w_gate_up, expert_b_gate_up,
                    expert_w_down, expert_b_down)
    h = h + jax.nn.sigmoid(rmsnorm(h, norm_ple) @ ple_gate_w) * (p_i @ ple_proj_w)
    return h


def setup_inputs(seed: int = 0) -> dict:
    key = jax.random.key(seed)
    ks = jax.random.split(key, 26)
    f32 = jnp.float32
    L = DEPTH

    def nrm(k, shape, scale):
        return jax.random.normal(k, shape, f32) * scale

    def gain(k, shape):
        return 1.0 + 0.05 * jax.random.normal(k, shape, f32)

    return {
        'x': nrm(ks[0], (BATCH, SEQ, D_MODEL), 1.0),
        'p': nrm(ks[1], (DEPTH, BATCH, SEQ, PLE_DIM), 1.0),
        'w_in': nrm(ks[2], (L, D_MODEL, D_IN), D_MODEL ** -0.5),
        'gla_gate_up': nrm(ks[3], (L, GLA_GATE_RANK, GLA_QK_W), GLA_GATE_RANK ** -0.5),
        'gla_gate_bias': nrm(ks[4], (L, GLA_QK_W), 0.1),
        'gla_out_norm': gain(ks[5], (L, GLA_DV)),
        'mla_q_norm': gain(ks[6], (L, MLA_Q_RANK)),
        'mla_kv_norm': gain(ks[7], (L, MLA_KV_RANK)),
        'mla_w_uq': nrm(ks[8], (L, MLA_Q_RANK, MLA_HEADS * (MLA_NOPE + MLA_ROPE)), MLA_Q_RANK ** -0.5),
        'mla_w_ukv': nrm(ks[9], (L, MLA_KV_RANK, MLA_HEADS * (MLA_NOPE + MLA_V)), MLA_KV_RANK ** -0.5),
        'w_branch_gla': nrm(ks[10], (L, GLA_V_W, D_MODEL), GLA_V_W ** -0.5),
        'w_branch_mla': nrm(ks[11], (L, MLA_V_W, D_MODEL), MLA_V_W ** -0.5),
        'w_out': nrm(ks[12], (L, D_MODEL, D_MODEL), D_MODEL ** -0.5),
        'norm_mix': gain(ks[13], (L, D_MODEL)),
        'norm_moe': gain(ks[14], (L, D_MODEL)),
        'router_w': nrm(ks[15], (L, D_MODEL, N_EXPERTS), D_MODEL ** -0.5),
        'router_b': nrm(ks[16], (L, N_EXPERTS), 0.01),
        'expert_w_gate_up': nrm(ks[17], (L, N_EXPERTS, D_MODEL, 2 * D_EXPERT), D_MODEL ** -0.5),
        'expert_b_gate_up': nrm(ks[18], (L, N_EXPERTS, 2 * D_EXPERT), 0.01),
        'expert_w_down': nrm(ks[19], (L, N_EXPERTS, D_EXPERT, D_MODEL), D_EXPERT ** -0.5),
        'expert_b_down': nrm(ks[20], (L, N_EXPERTS, D_MODEL), 0.01),
        'norm_ple': gain(ks[21], (L, D_MODEL)),
        'ple_gate_w': nrm(ks[22], (L, D_MODEL, D_MODEL), D_MODEL ** -0.5),
        'ple_proj_w': nrm(ks[23], (L, PLE_DIM, D_MODEL), PLE_DIM ** -0.5),
        'final_norm': gain(ks[24], (D_MODEL,)),
    }


def reference(x, p, w_in, gla_gate_up, gla_gate_bias, gla_out_norm, mla_q_norm, mla_kv_norm,
              mla_w_uq, mla_w_ukv, w_branch_gla, w_branch_mla, w_out, norm_mix, norm_moe,
              router_w, router_b, expert_w_gate_up, expert_b_gate_up, expert_w_down, expert_b_down,
              norm_ple, ple_gate_w, ple_proj_w, final_norm):
    S = x.shape[1]
    pos = jnp.arange(S, dtype=jnp.float32)
    inv_freq = ROPE_THETA ** (-jnp.arange(0, MLA_ROPE, 2, dtype=jnp.float32) / MLA_ROPE)
    ang = pos[:, None] * inv_freq[None, :]
    cos, sin = jnp.cos(ang), jnp.sin(ang)
    h = x
    for i in range(DEPTH):
        h = hybrid_layer(h, p[i], w_in[i], gla_gate_up[i], gla_gate_bias[i], gla_out_norm[i],
                         mla_q_norm[i], mla_kv_norm[i], mla_w_uq[i], mla_w_ukv[i],
                         w_branch_gla[i], w_branch_mla[i], w_out[i], norm_mix[i], norm_moe[i],
                         router_w[i], router_b[i], expert_w_gate_up[i], expert_b_gate_up[i],
                         expert_w_down[i], expert_b_down[i], norm_ple[i], ple_gate_w[i],
                         ple_proj_w[i], cos, sin)
    return rmsnorm(h, final_norm)
```

```python
import functools

import jax
import jax.numpy as jnp
from jax import lax
from jax.experimental import pallas as pl
from jax.experimental.pallas import tpu as pltpu

F32 = jnp.float32
BF16 = jnp.bfloat16
U32 = jnp.uint32
I32 = jnp.int32

CHUNK = 64
EPS = 1e-6
GLA_HEADS = 8
GLA_TAU = 16.0
MLA_HEADS = 16
MLA_NOPE = 128
MLA_ROPE = 64
MLA_V = 128
ROPE_THETA = 10000.0
TOP_K = 4
SWIGLU_LIMIT = 7.0
SWIGLU_ALPHA = 1.702

LANE = 128
VMEM_LIMIT = 56 * 2 ** 20
NEG_BIG = -1e30


def _tile(n, pref):
    if n <= pref:
        return n
    t = pref
    while t >= 8:
        if n % t == 0 and t % 8 == 0:
            return t
        t -= 8
    return n


def _cp(sem):
    return pltpu.CompilerParams(dimension_semantics=sem, vmem_limit_bytes=VMEM_LIMIT)


def _dot(a, b):
    return jnp.dot(a, b, preferred_element_type=F32)


def _dot_nt(a, b):
    return lax.dot_general(a, b, (((1,), (1,)), ((), ())), preferred_element_type=F32)


def _dot_tn(a, b):
    return lax.dot_general(a, b, (((0,), (0,)), ((), ())), preferred_element_type=F32)


def _rms(x, g):
    return x * lax.rsqrt(jnp.mean(x * x, axis=-1, keepdims=True) + EPS) * g


def _rms_body(x_ref, g_ref, o_ref):
    o_ref[...] = _rms(x_ref[...].astype(F32), g_ref[...]).astype(o_ref.dtype)


def rms_norm(x, g, *, col_block=0, width=None, out_dtype=BF16, tm=512):
    m = x.shape[0]
    width = width or x.shape[1]
    tm = _tile(m, tm)
    return pl.pallas_call(
        _rms_body,
        grid=(m // tm,),
        in_specs=[pl.BlockSpec((tm, width), lambda i: (i, col_block)),
                  pl.BlockSpec((1, width), lambda i: (0, 0))],
        out_specs=pl.BlockSpec((tm, width), lambda i: (i, 0)),
        out_shape=jax.ShapeDtypeStruct((m, width), out_dtype),
        compiler_params=_cp(("parallel",)),
        name="rms_norm",
    )(x, g.reshape(1, -1).astype(F32))


def _mm_body(a_ref, w_ref, o_ref):
    o_ref[...] = _dot(a_ref[...], w_ref[...]).astype(o_ref.dtype)


def matmul(a, w, *, out_dtype, tm=1024, tn=1024, name="matmul"):
    m, k = a.shape
    n = w.shape[1]
    tm, tn = _tile(m, tm), _tile(n, tn)
    return pl.pallas_call(
        _mm_body,
        grid=(m // tm, n // tn),
        in_specs=[pl.BlockSpec((tm, k), lambda i, j: (i, 0)),
                  pl.BlockSpec((k, tn), lambda i, j: (0, j))],
        out_specs=pl.BlockSpec((tm, tn), lambda i, j: (i, j)),
        out_shape=jax.ShapeDtypeStruct((m, n), out_dtype),
        compiler_params=_cp(("parallel", "arbitrary")),
        name=name,
    )(a, w)


def _gla_body(q_ref, k_ref, v_ref, gg_ref, gf_ref, gup_ref, gb_ref, on_ref, tri_ref, blk_ref,
              o_ref, st_ref, ob_ref, *, nchunk, scale):
    @pl.when(pl.program_id(2) == 0)
    def _():
        st_ref[...] = jnp.zeros_like(st_ref)

    logit = _dot(gf_ref[...], gup_ref[...]) + gb_ref[...]
    la = (jnp.minimum(logit, 0.0) - jnp.log(1.0 + jnp.exp(-jnp.abs(logit)))) * (1.0 / GLA_TAU)
    la_hi = la.astype(BF16)
    la_lo = (la - la_hi.astype(F32)).astype(BF16)
    tri = tri_ref[...]
    blk = blk_ref[...]
    b = _dot(tri, la_hi) + _dot(tri, la_lo)
    bl = _dot(blk, la_hi) + _dot(blk, la_lo)
    q = q_ref[...].astype(F32)
    k = k_ref[...].astype(F32)
    qd = (q * scale * jnp.exp(b)).astype(BF16)
    kd = (k * jnp.exp(-b)).astype(BF16)
    ke = (k * jnp.exp(bl - b)).astype(BF16)
    dec = jnp.exp(bl)
    row = lax.broadcasted_iota(I32, (CHUNK, CHUNK), 0)
    col = lax.broadcasted_iota(I32, (CHUNK, CHUNK), 1)
    causal = row >= col
    for c in range(nchunk):
        lo = c * CHUNK
        qc = qd[lo:lo + CHUNK]
        kc = kd[lo:lo + CHUNK]
        kec = ke[lo:lo + CHUNK]
        vc = v_ref[lo:lo + CHUNK, :]
        s = jnp.where(causal, _dot_nt(qc, kc), 0.0).astype(BF16)
        st = st_ref[...]
        ob_ref[lo:lo + CHUNK, :] = _dot(s, vc) + _dot_nt(qc, st.astype(BF16))
        st_ref[...] = st * dec[lo:lo + 1, :] + _dot_tn(vc, kec)
    o = ob_ref[...]
    o = _rms(o, on_ref[...])
    g = gg_ref[...].astype(F32)
    o_ref[...] = (o * (g * jax.nn.sigmoid(g))).astype(o_ref.dtype)


def gla_mixer(proj, lay, gate_up, gate_bias, out_norm, batch, seq):
    heads = GLA_HEADS
    dk = gate_up.shape[1] // heads
    dv = out_norm.shape[0]
    rank = gate_up.shape[0]
    lb = _tile(seq, 512)
    nchunk = lb // CHUNK
    nsb = seq // lb
    t = batch * seq
    gup = jnp.zeros((LANE, heads * dk), F32).at[:rank].set(gate_up).astype(BF16)
    r = jnp.arange(lb)
    same = (r[:, None] // CHUNK) == (r[None, :] // CHUNK)
    tri = (same & (r[None, :] <= r[:, None])).astype(BF16)
    blk = same.astype(BF16)
    qb, kb = lay["gq"] // dk, lay["gk"] // dk
    vb, gb = lay["gv"] // dv, lay["gg"] // dv
    fb = lay["gf"] // LANE
    rowmap = lambda b, h, s: b * nsb + s
    return pl.pallas_call(
        functools.partial(_gla_body, nchunk=nchunk, scale=dk ** -0.5),
        grid=(batch, heads, nsb),
        in_specs=[
            pl.BlockSpec((lb, dk), lambda b, h, s: (rowmap(b, h, s), qb + h)),
            pl.BlockSpec((lb, dk), lambda b, h, s: (rowmap(b, h, s), kb + h)),
            pl.BlockSpec((lb, dv), lambda b, h, s: (rowmap(b, h, s), vb + h)),
            pl.BlockSpec((lb, dv), lambda b, h, s: (rowmap(b, h, s), gb + h)),
            pl.BlockSpec((lb, LANE), lambda b, h, s: (rowmap(b, h, s), fb)),
            pl.BlockSpec((LANE, dk), lambda b, h, s: (0, h)),
            pl.BlockSpec((1, dk), lambda b, h, s: (0, h)),
            pl.BlockSpec((1, dv), lambda b, h, s: (0, 0)),
            pl.BlockSpec((lb, lb), lambda b, h, s: (0, 0)),
            pl.BlockSpec((lb, lb), lambda b, h, s: (0, 0)),
        ],
        out_specs=pl.BlockSpec((lb, dv), lambda b, h, s: (rowmap(b, h, s), h)),
        out_shape=jax.ShapeDtypeStruct((t, heads * dv), BF16),
        scratch_shapes=[pltpu.VMEM((dv, dk), F32), pltpu.VMEM((lb, dv), F32)],
        compiler_params=_cp(("parallel", "parallel", "arbitrary")),
        name="gla_mixer",
    )(proj, proj, proj, proj, proj, gup, gate_bias.reshape(1, -1).astype(F32),
      out_norm.reshape(1, -1).astype(F32), tri, blk)


def _q_up_body(a_ref, w_ref, c_ref, s_ref, o_ref, *, hpt, scale):
    acc = _dot(a_ref[...], w_ref[...])
    c = c_ref[...]
    s = s_ref[...]
    for h in range(hpt):
        src = h * 3 * LANE
        dst = h * 2 * LANE
        o_ref[:, dst:dst + LANE] = (acc[:, src:src + LANE] * scale).astype(o_ref.dtype)
        pe = acc[:, src + LANE:src + 2 * LANE] * c + acc[:, src + 2 * LANE:src + 3 * LANE] * s
        o_ref[:, dst + LANE:dst + 2 * LANE] = (pe * scale).astype(o_ref.dtype)


def _k_up_body(a_ref, w_ref, kra_ref, krb_ref, c_ref, s_ref, o_ref, *, hpt):
    acc = _dot(a_ref[...], w_ref[...])
    kpe = (kra_ref[...].astype(F32) * c_ref[...] + krb_ref[...].astype(F32) * s_ref[...]).astype(o_ref.dtype)
    for h in range(hpt):
        o_ref[:, 2 * h * LANE:(2 * h + 1) * LANE] = acc[:, h * LANE:(h + 1) * LANE].astype(o_ref.dtype)
        o_ref[:, (2 * h + 1) * LANE:(2 * h + 2) * LANE] = kpe


def _swap_halves(w):
    half = w.shape[-1] // 2
    return jnp.concatenate([w[..., half:], w[..., :half]], axis=-1)


def _pad_lane(w):
    pad = LANE - w.shape[-1]
    return jnp.pad(w, [(0, 0)] * (w.ndim - 1) + [(0, pad)])


def mla_mixer(proj, lay, q_norm, kv_norm, w_uq, w_ukv, batch, seq):
    heads = MLA_HEADS
    t = batch * seq
    qr, kvr = q_norm.shape[0], kv_norm.shape[0]
    assert MLA_NOPE == LANE and MLA_V == LANE and MLA_ROPE <= LANE
    scale = (MLA_NOPE + MLA_ROPE) ** -0.5
    tm = _tile(seq, 1024)
    nrb = seq // tm
    hpt = 4 if heads % 4 == 0 else 1

    pos = jnp.arange(seq, dtype=F32)
    inv_freq = ROPE_THETA ** (-jnp.arange(0, MLA_ROPE, 2, dtype=F32) / MLA_ROPE)
    ang = pos[:, None] * inv_freq[None, :]
    cos, sin = jnp.cos(ang), jnp.sin(ang)
    ctab = _pad_lane(jnp.concatenate([cos, cos], axis=-1))
    stab = _pad_lane(jnp.concatenate([-sin, sin], axis=-1))

    cqn = rms_norm(proj, q_norm, col_block=lay["cq"] // qr, width=qr)
    ckvn = rms_norm(proj, kv_norm, col_block=lay["ckv"] // kvr, width=kvr)

    wq = w_uq.reshape(qr, heads, MLA_NOPE + MLA_ROPE)
    wq_pe = wq[..., MLA_NOPE:]
    wq_r = jnp.concatenate([wq[..., :MLA_NOPE], _pad_lane(wq_pe), _pad_lane(_swap_halves(wq_pe))],
                           axis=-1).reshape(qr, heads * 3 * LANE).astype(BF16)
    q_cat = pl.pallas_call(
        functools.partial(_q_up_body, hpt=hpt, scale=scale),
        grid=(t // tm, heads // hpt),
        in_specs=[pl.BlockSpec((tm, qr), lambda i, j: (i, 0)),
                  pl.BlockSpec((qr, hpt * 3 * LANE), lambda i, j: (0, j)),
                  pl.BlockSpec((tm, LANE), lambda i, j: (i % nrb, 0)),
                  pl.BlockSpec((tm, LANE), lambda i, j: (i % nrb, 0))],
        out_specs=pl.BlockSpec((tm, hpt * 2 * LANE), lambda i, j: (i, j)),
        out_shape=jax.ShapeDtypeStruct((t, heads * 2 * LANE), BF16),
        compiler_params=_cp(("parallel", "arbitrary")),
        name="mla_q_up",
    )(cqn, wq_r, ctab, stab)

    wkv = w_ukv.reshape(kvr, heads, MLA_NOPE + MLA_V)
    wk = wkv[..., :MLA_NOPE].reshape(kvr, heads * MLA_NOPE).astype(BF16)
    wv = wkv[..., MLA_NOPE:].reshape(kvr, heads * MLA_V).astype(BF16)
    ka, kb = lay["krA"] // LANE, lay["krB"] // LANE
    k_cat = pl.pallas_call(
        functools.partial(_k_up_body, hpt=hpt),
        grid=(t // tm, heads // hpt),
        in_specs=[pl.BlockSpec((tm, kvr), lambda i, j: (i, 0)),
                  pl.BlockSpec((kvr, hpt * LANE), lambda i, j: (0, j)),
                  pl.BlockSpec((tm, LANE), lambda i, j: (i, ka)),
                  pl.BlockSpec((tm, LANE), lambda i, j: (i, kb)),
                  pl.BlockSpec((tm, LANE), lambda i, j: (i % nrb, 0)),
                  pl.BlockSpec((tm, LANE), lambda i, j: (i % nrb, 0))],
        out_specs=pl.BlockSpec((tm, hpt * 2 * LANE), lambda i, j: (i, j)),
        out_shape=jax.ShapeDtypeStruct((t, heads * 2 * LANE), BF16),
        compiler_params=_cp(("parallel", "arbitrary")),
        name="mla_k_up",
    )(ckvn, wk, proj, proj, ctab, stab)
    v = matmul(ckvn, wv, out_dtype=BF16, name="mla_v_up")
    return mla_attention(q_cat, k_cat, v, batch, seq)


def _mla_attn_body(qt_ref, kt_ref, q_ref, k_ref, v_ref, o_ref, m_ref, l_ref, acc_ref, *, tq):
    step = pl.program_id(2)
    qi = qt_ref[step]
    ki = kt_ref[step]

    @pl.when(ki == 0)
    def _():
        m_ref[...] = jnp.full_like(m_ref, NEG_BIG)
        l_ref[...] = jnp.zeros_like(l_ref)
        acc_ref[...] = jnp.zeros_like(acc_ref)

    def update(s):
        m_prev = m_ref[...]
        m_new = jnp.maximum(m_prev, jnp.max(s, axis=-1, keepdims=True))
        alpha = jnp.exp(m_prev - m_new)
        p = jnp.exp(s - m_new)
        l_ref[...] = alpha * l_ref[...] + jnp.sum(p, axis=-1, keepdims=True)
        acc_ref[...] = alpha * acc_ref[...] + _dot(p.astype(BF16), v_ref[...])
        m_ref[...] = m_new

    @pl.when(ki < qi)
    def _():
        update(_dot_nt(q_ref[...], k_ref[...]))

    @pl.when(ki == qi)
    def _():
        s = _dot_nt(q_ref[...], k_ref[...])
        qc = lax.broadcasted_iota(I32, (tq, tq), 0) // CHUNK
        kc = lax.broadcasted_iota(I32, (tq, tq), 1) // CHUNK
        update(jnp.where(kc <= qc, s, NEG_BIG))
        o_ref[...] = (acc_ref[...] / l_ref[...]).astype(o_ref.dtype)


def mla_attention(q_cat, k_cat, v, batch, seq):
    heads = MLA_HEADS
    t = batch * seq
    tq = _tile(seq, 512)
    nq = seq // tq
    qt, kt = [], []
    for qi in range(nq):
        for ki in range(qi + 1):
            qt.append(qi)
            kt.append(ki)
    qt = jnp.asarray(qt, I32)
    kt = jnp.asarray(kt, I32)
    grid_spec = pltpu.PrefetchScalarGridSpec(
        num_scalar_prefetch=2,
        grid=(batch, heads, int(qt.shape[0])),
        in_specs=[pl.BlockSpec((tq, 2 * LANE), lambda b, h, s, qt, kt: (b * nq + qt[s], h)),
                  pl.BlockSpec((tq, 2 * LANE), lambda b, h, s, qt, kt: (b * nq + kt[s], h)),
                  pl.BlockSpec((tq, LANE), lambda b, h, s, qt, kt: (b * nq + kt[s], h))],
        out_specs=pl.BlockSpec((tq, LANE), lambda b, h, s, qt, kt: (b * nq + qt[s], h)),
        scratch_shapes=[pltpu.VMEM((tq, 1), F32), pltpu.VMEM((tq, 1), F32), pltpu.VMEM((tq, LANE), F32)],
    )
    return pl.pallas_call(
        functools.partial(_mla_attn_body, tq=tq),
        grid_spec=grid_spec,
        out_shape=jax.ShapeDtypeStruct((t, heads * MLA_V), BF16),
        compiler_params=_cp(("parallel", "parallel", "arbitrary")),
        name="mla_attention",
    )(qt, kt, q_cat, k_cat, v)


def _merge_body(a1_ref, w1_ref, a2_ref, w2_ref, g1_ref, g2_ref, o_ref):
    y1 = _dot(a1_ref[...], w1_ref[...])
    y2 = _dot(a2_ref[...], w2_ref[...])
    o_ref[...] = (jax.nn.sigmoid(g1_ref[...].astype(F32)) * y1
                  + jax.nn.sigmoid(g2_ref[...].astype(F32)) * y2).astype(o_ref.dtype)


def merge_branches(o_gla, o_mla, w_gla, w_mla, proj, lay, d_model):
    t = o_gla.shape[0]
    tm = _tile(t, 512)
    tn = lay["gate_tile"]
    g1, g2 = lay["gate_gla"] // tn, lay["gate_mla"] // tn
    k1, k2 = o_gla.shape[1], o_mla.shape[1]
    return pl.pallas_call(
        _merge_body,
        grid=(t // tm, d_model // tn),
        in_specs=[pl.BlockSpec((tm, k1), lambda i, j: (i, 0)),
                  pl.BlockSpec((k1, tn), lambda i, j: (0, j)),
                  pl.BlockSpec((tm, k2), lambda i, j: (i, 0)),
                  pl.BlockSpec((k2, tn), lambda i, j: (0, j)),
                  pl.BlockSpec((tm, tn), lambda i, j: (i, g1 + j)),
                  pl.BlockSpec((tm, tn), lambda i, j: (i, g2 + j))],
        out_specs=pl.BlockSpec((tm, tn), lambda i, j: (i, j)),
        out_shape=jax.ShapeDtypeStruct((t, d_model), BF16),
        compiler_params=_cp(("parallel", "arbitrary")),
        name="merge_branches",
    )(o_gla, w_gla, o_mla, w_mla, proj, proj)


def _mm_res_body(a_ref, w_ref, r_ref, o_ref):
    o_ref[...] = r_ref[...] + _dot(a_ref[...], w_ref[...])


def matmul_residual(a, w, res, *, tm=512, tn=1024, name="matmul_residual"):
    m, k = a.shape
    n = w.shape[1]
    tm, tn = _tile(m, tm), _tile(n, tn)
    return pl.pallas_call(
        _mm_res_body,
        grid=(m // tm, n // tn),
        in_specs=[pl.BlockSpec((tm, k), lambda i, j: (i, 0)),
                  pl.BlockSpec((k, tn), lambda i, j: (0, j)),
                  pl.BlockSpec((tm, tn), lambda i, j: (i, j))],
        out_specs=pl.BlockSpec((tm, tn), lambda i, j: (i, j)),
        out_shape=jax.ShapeDtypeStruct((m, n), F32),
        compiler_params=_cp(("parallel", "arbitrary")),
        name=name,
    )(a, w, res)


def _pack_bf16_pairs(x):
    n = x.shape[1] // 2
    bits = lax.bitcast_convert_type(x.astype(BF16).astype(F32), U32)
    return (bits[:, :n] >> 16) | (bits[:, n:] & jnp.uint32(0xFFFF0000))


def _unpack_bf16_pairs(w, out_ref):
    n = w.shape[1]
    out_ref[:, :n] = lax.bitcast_convert_type(w << 16, F32).astype(BF16)
    out_ref[:, n:] = lax.bitcast_convert_type(w & jnp.uint32(0xFFFF0000), F32).astype(BF16)


def _router_body(h_ref, g_ref, rw_ref, rb_ref, tri_ref, xp_ref, idx_ref, rank_ref, cw_ref, cnt_ref, carry_ref):
    @pl.when(pl.program_id(0) == 0)
    def _():
        carry_ref[...] = jnp.zeros_like(carry_ref)

    hn = _rms(h_ref[...], g_ref[...])
    xp_ref[...] = _pack_bf16_pairs(hn)
    logits = jnp.dot(hn, rw_ref[...], precision=lax.Precision.HIGHEST, preferred_element_type=F32) + rb_ref[...]
    tm = logits.shape[0]
    lane = lax.broadcasted_iota(I32, (tm, LANE), 1)
    work = logits
    vals, sels, idxs = [], [], []
    for _ in range(TOP_K):
        mx = jnp.max(work, axis=-1, keepdims=True)
        ix = jnp.min(jnp.where(work == mx, lane, LANE), axis=-1, keepdims=True)
        sel = lane == ix
        work = jnp.where(sel, -jnp.inf, work)
        vals.append(mx)
        sels.append(sel)
        idxs.append(ix)
    es = [jnp.exp(v - vals[0]) for v in vals]
    den = es[0]
    for e in es[1:]:
        den = den + e
    hot = jnp.zeros((tm, LANE), F32)
    for sel in sels:
        hot = hot + sel.astype(F32)
    carry = carry_ref[0:1, :]
    rank = _dot(tri_ref[...], hot.astype(BF16)) + carry
    idx_out = jnp.zeros((tm, LANE), I32)
    rank_out = jnp.zeros((tm, LANE), F32)
    cw_out = jnp.zeros((tm, LANE), F32)
    for k in range(TOP_K):
        rk = jnp.sum(jnp.where(sels[k], rank, 0.0), axis=-1, keepdims=True)
        idx_out = jnp.where(lane == k, idxs[k], idx_out)
        rank_out = jnp.where(lane == k, rk, rank_out)
        cw_out = jnp.where(lane == k, es[k] / den, cw_out)
    idx_ref[...] = idx_out
    rank_ref[...] = rank_out.astype(I32)
    cw_ref[...] = cw_out
    new_carry = carry + jnp.sum(hot, axis=0, keepdims=True)
    carry_ref[...] = jnp.broadcast_to(new_carry, carry_ref.shape)
    cnt_ref[...] = jnp.broadcast_to(new_carry, cnt_ref.shape)


def moe_router(h, norm_g, router_w, router_b, tm):
    t, d = h.shape
    ne = router_w.shape[1]
    assert ne <= LANE and d % (2 * LANE) == 0
    rw = jnp.zeros((d, LANE), F32).at[:, :ne].set(router_w)
    rb = jnp.full((1, LANE), NEG_BIG, F32).at[0, :ne].set(router_b)
    r = jnp.arange(tm)
    tri = (r[None, :] < r[:, None]).astype(BF16)
    row = lambda i: (i, 0)
    const = lambda i: (0, 0)
    return pl.pallas_call(
        _router_body,
        grid=(t // tm,),
        in_specs=[pl.BlockSpec((tm, d), row), pl.BlockSpec((1, d), const),
                  pl.BlockSpec((d, LANE), const), pl.BlockSpec((1, LANE), const),
                  pl.BlockSpec((tm, tm), const)],
        out_specs=[pl.BlockSpec((tm, d // 2), row), pl.BlockSpec((tm, LANE), row),
                   pl.BlockSpec((tm, LANE), row), pl.BlockSpec((tm, LANE), row),
                   pl.BlockSpec((8, LANE), const)],
        out_shape=[jax.ShapeDtypeStruct((t, d // 2), U32), jax.ShapeDtypeStruct((t, LANE), I32),
                   jax.ShapeDtypeStruct((t, LANE), I32), jax.ShapeDtypeStruct((t, LANE), F32),
                   jax.ShapeDtypeStruct((8, LANE), F32)],
        scratch_shapes=[pltpu.VMEM((8, LANE), F32)],
        compiler_params=_cp(("arbitrary",)),
        name="moe_router",
    )(h, norm_g.reshape(1, -1).astype(F32), rw, rb, tri)


def _scatter_body(pos_ref, x_ref, xs_in_ref, xs_ref, sem, *, tm):
    del xs_in_ref
    base = pl.program_id(0) * (tm * TOP_K)

    def issue(r, carry):
        for k in range(TOP_K):
            p = pos_ref[base + r * TOP_K + k]
            pltpu.make_async_copy(x_ref.at[pl.ds(r, 1)], xs_ref.at[pl.ds(p, 1)], sem).start()
        return carry

    lax.fori_loop(0, tm, issue, 0)
    for _ in range(TOP_K):
        pltpu.make_async_copy(x_ref, xs_ref.at[pl.ds(0, tm)], sem).wait()


def moe_scatter(xp, pos_flat, n_rows, tm):
    t, w = xp.shape
    grid_spec = pltpu.PrefetchScalarGridSpec(
        num_scalar_prefetch=1,
        grid=(t // tm,),
        in_specs=[pl.BlockSpec((tm, w), lambda i, pos: (i, 0)),
                  pl.BlockSpec(memory_space=pl.ANY)],
        out_specs=pl.BlockSpec(memory_space=pl.ANY),
        scratch_shapes=[pltpu.SemaphoreType.DMA(())],
    )
    return pl.pallas_call(
        functools.partial(_scatter_body, tm=tm),
        grid_spec=grid_spec,
        out_shape=jax.ShapeDtypeStruct((n_rows, w), xp.dtype),
        input_output_aliases={2: 0},
        compiler_params=_cp(("arbitrary",)),
        name="moe_scatter",
    )(pos_flat, xp, jnp.zeros((n_rows, w), xp.dtype))


def _moe_up_body(te_ref, nu_ref, xs_ref, w_ref, b_ref, o_ref, xb_ref, *, de):
    i = pl.program_id(0)

    @pl.when(i < nu_ref[0])
    def _():
        _unpack_bf16_pairs(xs_ref[...], xb_ref)
        gu = _dot(xb_ref[...], w_ref[...]) + b_ref[...]
        gate = jnp.minimum(gu[:, :de], SWIGLU_LIMIT)
        up = jnp.clip(gu[:, de:], -SWIGLU_LIMIT, SWIGLU_LIMIT)
        o_ref[...] = ((up + 1.0) * (gate * jax.nn.sigmoid(SWIGLU_ALPHA * gate))).astype(o_ref.dtype)

    @pl.when(i >= nu_ref[0])
    def _():
        o_ref[...] = jnp.zeros_like(o_ref)


def _moe_down_body(te_ref, nu_ref, a_ref, w_ref, b_ref, o_ref):
    i = pl.program_id(0)

    @pl.when(i < nu_ref[0])
    def _():
        o_ref[...] = _dot(a_ref[...], w_ref[...]) + b_ref[...]

    @pl.when(i >= nu_ref[0])
    def _():
        o_ref[...] = jnp.zeros_like(o_ref)


def moe_experts(xs, tile_expert, n_used, w_gu, b_gu, w_dn, b_dn, tm):
    n_rows, half = xs.shape
    d = 2 * half
    ne, _, de2 = w_gu.shape
    de = de2 // 2
    n_tiles = n_rows // tm
    act = pl.pallas_call(
        functools.partial(_moe_up_body, de=de),
        grid_spec=pltpu.PrefetchScalarGridSpec(
            num_scalar_prefetch=2,
            grid=(n_tiles,),
            in_specs=[pl.BlockSpec((tm, half), lambda i, te, nu: (i, 0)),
                      pl.BlockSpec((None, d, de2), lambda i, te, nu: (te[i], 0, 0)),
                      pl.BlockSpec((None, 1, de2), lambda i, te, nu: (te[i], 0, 0))],
            out_specs=pl.BlockSpec((tm, de), lambda i, te, nu: (i, 0)),
            scratch_shapes=[pltpu.VMEM((tm, d), BF16)],
        ),
        out_shape=jax.ShapeDtypeStruct((n_rows, de), BF16),
        compiler_params=_cp(("arbitrary",)),
        name="moe_up",
    )(tile_expert, n_used, xs, w_gu, b_gu.reshape(ne, 1, de2))
    return pl.pallas_call(
        _moe_down_body,
        grid_spec=pltpu.PrefetchScalarGridSpec(
            num_scalar_prefetch=2,
            grid=(n_tiles,),
            in_specs=[pl.BlockSpec((tm, de), lambda i, te, nu: (i, 0)),
                      pl.BlockSpec((None, de, d), lambda i, te, nu: (te[i], 0, 0)),
                      pl.BlockSpec((None, 1, d), lambda i, te, nu: (te[i], 0, 0))],
            out_specs=pl.BlockSpec((tm, d), lambda i, te, nu: (i, 0)),
        ),
        out_shape=jax.ShapeDtypeStruct((n_rows, d), F32),
        compiler_params=_cp(("arbitrary",)),
        name="moe_down",
    )(tile_expert, n_used, act, w_dn, b_dn.reshape(ne, 1, d))


def _combine_body(pos_ref, h_ref, cw_ref, g_ref, ys_ref, h2_ref, hn_ref, buf_ref, sem, *, tm):
    base = pl.program_id(0) * (tm * TOP_K)

    def issue(r, carry):
        for k in range(TOP_K):
            p = pos_ref[base + r * TOP_K + k]
            pltpu.make_async_copy(ys_ref.at[pl.ds(p, 1)], buf_ref.at[k, pl.ds(r, 1)], sem).start()
        return carry

    lax.fori_loop(0, tm, issue, 0)
    for k in range(TOP_K):
        pltpu.make_async_copy(ys_ref.at[pl.ds(0, tm)], buf_ref.at[k], sem).wait()
    acc = h_ref[...]
    cw = cw_ref[...]
    for k in range(TOP_K):
        acc = acc + cw[:, k:k + 1] * buf_ref[k]
    h2_ref[...] = acc
    hn_ref[...] = _rms(acc, g_ref[...]).astype(hn_ref.dtype)


def moe_combine(h, cw, ys, pos_flat, norm_g, tm):
    t, d = h.shape
    grid_spec = pltpu.PrefetchScalarGridSpec(
        num_scalar_prefetch=1,
        grid=(t // tm,),
        in_specs=[pl.BlockSpec((tm, d), lambda i, pos: (i, 0)),
                  pl.BlockSpec((tm, LANE), lambda i, pos: (i, 0)),
                  pl.BlockSpec((1, d), lambda i, pos: (0, 0)),
                  pl.BlockSpec(memory_space=pl.ANY)],
        out_specs=[pl.BlockSpec((tm, d), lambda i, pos: (i, 0)),
                   pl.BlockSpec((tm, d), lambda i, pos: (i, 0))],
        scratch_shapes=[pltpu.VMEM((TOP_K, tm, d), F32), pltpu.SemaphoreType.DMA(())],
    )
    return pl.pallas_call(
        functools.partial(_combine_body, tm=tm),
        grid_spec=grid_spec,
        out_shape=[jax.ShapeDtypeStruct((t, d), F32), jax.ShapeDtypeStruct((t, d), BF16)],
        compiler_params=_cp(("arbitrary",)),
        name="moe_combine",
    )(pos_flat, h, cw, norm_g.reshape(1, -1).astype(F32), ys)


def _ple_body(a_ref, w_ref, p_ref, wp_ref, h_ref, o_ref):
    gate = _dot(a_ref[...], w_ref[...])
    pp = _dot(p_ref[...].astype(BF16), wp_ref[...])
    o_ref[...] = h_ref[...] + jax.nn.sigmoid(gate) * pp


def ple_gate(hn, w_gate, p, w_proj, h, *, tm=512, tn=1024):
    t, d = h.shape
    pd = p.shape[1]
    tm, tn = _tile(t, tm), _tile(d, tn)
    return pl.pallas_call(
        _ple_body,
        grid=(t // tm, d // tn),
        in_specs=[pl.BlockSpec((tm, d), lambda i, j: (i, 0)),
                  pl.BlockSpec((d, tn), lambda i, j: (0, j)),
                  pl.BlockSpec((tm, pd), lambda i, j: (i, 0)),
                  pl.BlockSpec((pd, tn), lambda i, j: (0, j)),
                  pl.BlockSpec((tm, tn), lambda i, j: (i, j))],
        out_specs=pl.BlockSpec((tm, tn), lambda i, j: (i, j)),
        out_shape=jax.ShapeDtypeStruct((t, d), F32),
        compiler_params=_cp(("parallel", "arbitrary")),
        name="ple_gate",
    )(hn, w_gate, p, w_proj, h)


def _proj_layout(d_model, qkw, vw, qr, kvr):
    lay, off = {}, 0
    for name, width in (("gq", qkw), ("gk", qkw), ("gv", vw), ("gg", vw), ("cq", qr), ("ckv", kvr),
                        ("krA", LANE), ("krB", LANE), ("gf", LANE)):
        lay[name] = off
        off += width
    gate_tile = _tile(d_model, 1024)
    off = -(-off // gate_tile) * gate_tile
    lay["gate_gla"] = off
    lay["gate_mla"] = off + d_model
    lay["total"] = off + 2 * d_model
    lay["gate_tile"] = gate_tile
    return lay


def _reorder_w_in(w_in, lay, qkw, vw, rank, qr, kvr, d_model):
    splits, off = [], 0
    for width in (qkw, qkw, vw, vw, rank, qr, kvr, MLA_ROPE, d_model, d_model):
        splits.append(w_in[:, off:off + width])
        off += width
    gq, gk, gv, gg, gf, cq, ckv, kr, g_gla, g_mla = splits
    front = jnp.concatenate([gq, gk, gv, gg, cq, ckv, _pad_lane(kr), _pad_lane(_swap_halves(kr)),
                             _pad_lane(gf)], axis=1)
    front = jnp.pad(front, ((0, 0), (0, lay["gate_gla"] - front.shape[1])))
    return jnp.concatenate([front, g_gla, g_mla], axis=1).astype(BF16)


def _layer(h, p, w_in, gla_gate_up, gla_gate_bias, gla_out_norm, mla_q_norm, mla_kv_norm, mla_w_uq,
           mla_w_ukv, w_branch_gla, w_branch_mla, w_out, norm_mix, norm_moe, router_w, router_b,
           expert_w_gate_up, expert_b_gate_up, expert_w_down, expert_b_down, norm_ple, ple_gate_w,
           ple_proj_w, batch, seq):
    t, d = h.shape
    rank, qkw = gla_gate_up.shape
    vw = w_branch_gla.shape[0]
    qr, kvr = mla_q_norm.shape[0], mla_kv_norm.shape[0]
    lay = _proj_layout(d, qkw, vw, qr, kvr)

    xn = rms_norm(h, norm_mix)
    proj = matmul(xn, _reorder_w_in(w_in, lay, qkw, vw, rank, qr, kvr, d), out_dtype=BF16,
                  tn=lay["gate_tile"], name="in_proj")
    o_gla = gla_mixer(proj, lay, gla_gate_up, gla_gate_bias, gla_out_norm, batch, seq)
    o_mla = mla_mixer(proj, lay, mla_q_norm, mla_kv_norm, mla_w_uq, mla_w_ukv, batch, seq)
    merged = merge_branches(o_gla, o_mla, w_branch_gla.astype(BF16), w_branch_mla.astype(BF16), proj, lay, d)
    h1 = matmul_residual(merged, w_out.astype(BF16), h, name="out_proj")

    ne = router_w.shape[1]
    tm_e = _tile(t, 256)
    xp, idx, rank_in, cw, cnt = moe_router(h1, norm_moe, router_w, router_b, tm_e)
    counts = cnt[0, :ne].astype(I32)
    padded = -(-counts // tm_e) * tm_e
    gend = jnp.cumsum(padded)
    gstart = gend - padded
    pos = (gstart[idx[:, :TOP_K]] + rank_in[:, :TOP_K]).reshape(-1)
    n_tiles = t * TOP_K // tm_e + ne
    tile_start = jnp.arange(n_tiles, dtype=I32) * tm_e
    tile_expert = jnp.minimum(jnp.sum((gend[None, :] <= tile_start[:, None]).astype(I32), axis=1), ne - 1)
    n_used = (gend[-1:] // tm_e).astype(I32)
    xs = moe_scatter(xp, pos, n_tiles * tm_e, tm_e)
    ys = moe_experts(xs, tile_expert, n_used, expert_w_gate_up.astype(BF16), expert_b_gate_up,
                     expert_w_down.astype(BF16), expert_b_down, tm_e)
    h2, hn2 = moe_combine(h1, cw, ys, pos, norm_ple, _tile(t, 128))

    return ple_gate(hn2, ple_gate_w.astype(BF16), p, ple_proj_w.astype(BF16), h2)


def kernel(x, p, w_in, gla_gate_up, gla_gate_bias, gla_out_norm, mla_q_norm, mla_kv_norm, mla_w_uq, mla_w_ukv, w_branch_gla, w_branch_mla, w_out, norm_mix, norm_moe, router_w, router_b, expert_w_gate_up, expert_b_gate_up, expert_w_down, expert_b_down, norm_ple, ple_gate_w, ple_proj_w, final_norm):
    batch, seq, d = x.shape
    h = x.reshape(batch * seq, d)
    for i in range(p.shape[0]):
        h = _layer(h, p[i].reshape(batch * seq, -1), w_in[i], gla_gate_up[i], gla_gate_bias[i],
                   gla_out_norm[i], mla_q_norm[i], mla_kv_norm[i], mla_w_uq[i], mla_w_ukv[i],
                   w_branch_gla[i], w_branch_mla[i], w_out[i], norm_mix[i], norm_moe[i], router_w[i],
                   router_b[i], expert_w_gate_up[i], expert_b_gate_up[i], expert_w_down[i],
                   expert_b_down[i], norm_ple[i], ple_gate_w[i], ple_proj_w[i], batch, seq)
    return rms_norm(h, final_norm, out_dtype=F32).reshape(batch, seq, d)
```

```python
import functools
import math

import jax
import jax.numpy as jnp
from jax import lax
from jax.experimental import pallas as pl
from jax.experimental.pallas import tpu as pltpu

F32 = jnp.float32
BF16 = jnp.bfloat16
U32 = jnp.uint32
I32 = jnp.int32

CHUNK = 64
EPS = 1e-6
GLA_HEADS = 8
GLA_TAU = 16.0
MLA_HEADS = 16
MLA_NOPE = 128
MLA_ROPE = 64
MLA_V = 128
ROPE_THETA = 10000.0
TOP_K = 4
SWIGLU_LIMIT = 7.0
SWIGLU_ALPHA = 1.702

LANE = 128
VMEM_LIMIT = 56 * 2 ** 20
NEG_BIG = -1e30


def _tile(n, pref):
    if n <= pref:
        return n
    t = pref
    while t >= 8:
        if n % t == 0 and t % 8 == 0:
            return t
        t -= 8
    return n


def _cp(sem):
    return pltpu.CompilerParams(dimension_semantics=sem, vmem_limit_bytes=VMEM_LIMIT)


def _dot(a, b):
    return jnp.dot(a, b, preferred_element_type=F32)


def _dot_nt(a, b):
    return lax.dot_general(a, b, (((1,), (1,)), ((), ())), preferred_element_type=F32)


def _dot_tn(a, b):
    return lax.dot_general(a, b, (((0,), (0,)), ((), ())), preferred_element_type=F32)


def _rms(x, g):
    return x * lax.rsqrt(jnp.mean(x * x, axis=-1, keepdims=True) + EPS) * g


def _rms_body(x_ref, g_ref, o_ref):
    o_ref[...] = _rms(x_ref[...].astype(F32), g_ref[...]).astype(o_ref.dtype)


def rms_norm(x, g, *, col_block=0, width=None, out_dtype=BF16, tm=512):
    m = x.shape[0]
    width = width or x.shape[1]
    tm = _tile(m, tm)
    return pl.pallas_call(
        _rms_body,
        grid=(m // tm,),
        in_specs=[pl.BlockSpec((tm, width), lambda i: (i, col_block)),
                  pl.BlockSpec((1, width), lambda i: (0, 0))],
        out_specs=pl.BlockSpec((tm, width), lambda i: (i, 0)),
        out_shape=jax.ShapeDtypeStruct((m, width), out_dtype),
        compiler_params=_cp(("parallel",)),
        name="rms_norm",
    )(x, g.reshape(1, -1).astype(F32))


def _mm_body(a_ref, w_ref, o_ref):
    o_ref[...] = _dot(a_ref[...], w_ref[...]).astype(o_ref.dtype)


def matmul(a, w, *, out_dtype, tm=1024, tn=1024, name="matmul"):
    m, k = a.shape
    n = w.shape[1]
    tm, tn = _tile(m, tm), _tile(n, tn)
    return pl.pallas_call(
        _mm_body,
        grid=(m // tm, n // tn),
        in_specs=[pl.BlockSpec((tm, k), lambda i, j: (i, 0)),
                  pl.BlockSpec((k, tn), lambda i, j: (0, j))],
        out_specs=pl.BlockSpec((tm, tn), lambda i, j: (i, j)),
        out_shape=jax.ShapeDtypeStruct((m, n), out_dtype),
        compiler_params=_cp(("parallel", "arbitrary")),
        name=name,
    )(a, w)


def _gla_body(q_ref, k_ref, v_ref, gg_ref, gf_ref, gup_ref, gb_ref, on_ref, tri_ref, blk_ref,
              o_ref, st_ref, ob_ref, *, nchunk, scale):
    @pl.when(pl.program_id(2) == 0)
    def _():
        st_ref[...] = jnp.zeros_like(st_ref)

    logit = _dot(gf_ref[...], gup_ref[...]) + gb_ref[...]
    la = (jnp.minimum(logit, 0.0) - jnp.log(1.0 + jnp.exp(-jnp.abs(logit)))) * (1.0 / GLA_TAU)
    la_hi = la.astype(BF16)
    la_lo = (la - la_hi.astype(F32)).astype(BF16)
    tri = tri_ref[...]
    blk = blk_ref[...]
    b = _dot(tri, la_hi) + _dot(tri, la_lo)
    bl = _dot(blk, la_hi) + _dot(blk, la_lo)
    q = q_ref[...].astype(F32)
    k = k_ref[...].astype(F32)
    qd = (q * scale * jnp.exp(b)).astype(BF16)
    kd = (k * jnp.exp(-b)).astype(BF16)
    ke = (k * jnp.exp(bl - b)).astype(BF16)
    dec = jnp.exp(bl)
    row = lax.broadcasted_iota(I32, (CHUNK, CHUNK), 0)
    col = lax.broadcasted_iota(I32, (CHUNK, CHUNK), 1)
    causal = row >= col
    for c in range(nchunk):
        lo = c * CHUNK
        qc = qd[lo:lo + CHUNK]
        kc = kd[lo:lo + CHUNK]
        kec = ke[lo:lo + CHUNK]
        vc = v_ref[lo:lo + CHUNK, :]
        s = jnp.where(causal, _dot_nt(qc, kc), 0.0).astype(BF16)
        st = st_ref[...]
        ob_ref[lo:lo + CHUNK, :] = _dot(s, vc) + _dot_nt(qc, st.astype(BF16))
        st_ref[...] = st * dec[lo:lo + 1, :] + _dot_tn(vc, kec)
    o = ob_ref[...]
    o = _rms(o, on_ref[...])
    g = gg_ref[...].astype(F32)
    o_ref[...] = (o * (g * jax.nn.sigmoid(g))).astype(o_ref.dtype)


def gla_mixer(proj, lay, gate_up, gate_bias, out_norm, batch, seq):
    heads = GLA_HEADS
    dk = gate_up.shape[1] // heads
    dv = out_norm.shape[0]
    rank = gate_up.shape[0]
    lb = _tile(seq, 512)
    nchunk = lb // CHUNK
    nsb = seq // lb
    t = batch * seq
    gup = jnp.zeros((LANE, heads * dk), F32).at[:rank].set(gate_up).astype(BF16)
    r = jnp.arange(lb)
    same = (r[:, None] // CHUNK) == (r[None, :] // CHUNK)
    tri = (same & (r[None, :] <= r[:, None])).astype(BF16)
    blk = same.astype(BF16)
    qb, kb = lay["gq"] // dk, lay["gk"] // dk
    vb, gb = lay["gv"] // dv, lay["gg"] // dv
    fb = lay["gf"] // LANE
    rowmap = lambda b, h, s: b * nsb + s
    return pl.pallas_call(
        functools.partial(_gla_body, nchunk=nchunk, scale=dk ** -0.5),
        grid=(batch, heads, nsb),
        in_specs=[
            pl.BlockSpec((lb, dk), lambda b, h, s: (rowmap(b, h, s), qb + h)),
            pl.BlockSpec((lb, dk), lambda b, h, s: (rowmap(b, h, s), kb + h)),
            pl.BlockSpec((lb, dv), lambda b, h, s: (rowmap(b, h, s), vb + h)),
            pl.BlockSpec((lb, dv), lambda b, h, s: (rowmap(b, h, s), gb + h)),
            pl.BlockSpec((lb, LANE), lambda b, h, s: (rowmap(b, h, s), fb)),
            pl.BlockSpec((LANE, dk), lambda b, h, s: (0, h)),
            pl.BlockSpec((1, dk), lambda b, h, s: (0, h)),
            pl.BlockSpec((1, dv), lambda b, h, s: (0, 0)),
            pl.BlockSpec((lb, lb), lambda b, h, s: (0, 0)),
            pl.BlockSpec((lb, lb), lambda b, h, s: (0, 0)),
        ],
        out_specs=pl.BlockSpec((lb, dv), lambda b, h, s: (rowmap(b, h, s), h)),
        out_shape=jax.ShapeDtypeStruct((t, heads * dv), BF16),
        scratch_shapes=[pltpu.VMEM((dv, dk), F32), pltpu.VMEM((lb, dv), F32)],
        compiler_params=_cp(("parallel", "parallel", "arbitrary")),
        name="gla_mixer",
    )(proj, proj, proj, proj, proj, gup, gate_bias.reshape(1, -1).astype(F32),
      out_norm.reshape(1, -1).astype(F32), tri, blk)


def _q_up_body(a_ref, w_ref, c_ref, s_ref, o_ref, *, hpt, scale):
    acc = _dot(a_ref[...], w_ref[...])
    c = c_ref[...]
    s = s_ref[...]
    for h in range(hpt):
        src = h * 3 * LANE
        dst = h * 2 * LANE
        o_ref[:, dst:dst + LANE] = (acc[:, src:src + LANE] * scale).astype(o_ref.dtype)
        pe = acc[:, src + LANE:src + 2 * LANE] * c + acc[:, src + 2 * LANE:src + 3 * LANE] * s
        o_ref[:, dst + LANE:dst + 2 * LANE] = (pe * scale).astype(o_ref.dtype)


def _k_up_body(a_ref, w_ref, kra_ref, krb_ref, c_ref, s_ref, o_ref, *, hpt):
    acc = _dot(a_ref[...], w_ref[...])
    kpe = (kra_ref[...].astype(F32) * c_ref[...] + krb_ref[...].astype(F32) * s_ref[...]).astype(o_ref.dtype)
    for h in range(hpt):
        o_ref[:, 2 * h * LANE:(2 * h + 1) * LANE] = acc[:, h * LANE:(h + 1) * LANE].astype(o_ref.dtype)
        o_ref[:, (2 * h + 1) * LANE:(2 * h + 2) * LANE] = kpe


def _v_up_body(a_ref, w_ref, o_ref, *, hpt):
    acc = _dot(a_ref[...], w_ref[...])
    lane = lax.broadcasted_iota(I32, (acc.shape[0], LANE), 1)
    ones_col = (lane == 0).astype(o_ref.dtype)
    for h in range(hpt):
        o_ref[:, 2 * h * LANE:(2 * h + 1) * LANE] = acc[:, h * LANE:(h + 1) * LANE].astype(o_ref.dtype)
        o_ref[:, (2 * h + 1) * LANE:(2 * h + 2) * LANE] = ones_col


def _swap_halves(w):
    half = w.shape[-1] // 2
    return jnp.concatenate([w[..., half:], w[..., :half]], axis=-1)


def _pad_lane(w):
    pad = LANE - w.shape[-1]
    return jnp.pad(w, [(0, 0)] * (w.ndim - 1) + [(0, pad)])


def mla_mixer(proj, lay, q_norm, kv_norm, w_uq, w_ukv, batch, seq):
    heads = MLA_HEADS
    t = batch * seq
    qr, kvr = q_norm.shape[0], kv_norm.shape[0]
    assert MLA_NOPE == LANE and MLA_V == LANE and MLA_ROPE <= LANE
    scale = (MLA_NOPE + MLA_ROPE) ** -0.5 * math.log2(math.e)
    tm = _tile(seq, 1024)
    nrb = seq // tm
    hpt = 4 if heads % 4 == 0 else 1

    pos = jnp.arange(seq, dtype=F32)
    inv_freq = ROPE_THETA ** (-jnp.arange(0, MLA_ROPE, 2, dtype=F32) / MLA_ROPE)
    ang = pos[:, None] * inv_freq[None, :]
    cos, sin = jnp.cos(ang), jnp.sin(ang)
    ctab = _pad_lane(jnp.concatenate([cos, cos], axis=-1))
    stab = _pad_lane(jnp.concatenate([-sin, sin], axis=-1))

    cqn = rms_norm(proj, q_norm, col_block=lay["cq"] // qr, width=qr)
    ckvn = rms_norm(proj, kv_norm, col_block=lay["ckv"] // kvr, width=kvr)

    wq = w_uq.reshape(qr, heads, MLA_NOPE + MLA_ROPE)
    wq_pe = wq[..., MLA_NOPE:]
    wq_r = jnp.concatenate([wq[..., :MLA_NOPE], _pad_lane(wq_pe), _pad_lane(_swap_halves(wq_pe))],
                           axis=-1).reshape(qr, heads * 3 * LANE).astype(BF16)
    q_cat = pl.pallas_call(
        functools.partial(_q_up_body, hpt=hpt, scale=scale),
        grid=(t // tm, heads // hpt),
        in_specs=[pl.BlockSpec((tm, qr), lambda i, j: (i, 0)),
                  pl.BlockSpec((qr, hpt * 3 * LANE), lambda i, j: (0, j)),
                  pl.BlockSpec((tm, LANE), lambda i, j: (i % nrb, 0)),
                  pl.BlockSpec((tm, LANE), lambda i, j: (i % nrb, 0))],
        out_specs=pl.BlockSpec((tm, hpt * 2 * LANE), lambda i, j: (i, j)),
        out_shape=jax.ShapeDtypeStruct((t, heads * 2 * LANE), BF16),
        compiler_params=_cp(("parallel", "arbitrary")),
        name="mla_q_up",
    )(cqn, wq_r, ctab, stab)

    wkv = w_ukv.reshape(kvr, heads, MLA_NOPE + MLA_V)
    wk = wkv[..., :MLA_NOPE].reshape(kvr, heads * MLA_NOPE).astype(BF16)
    wv = wkv[..., MLA_NOPE:].reshape(kvr, heads * MLA_V).astype(BF16)
    ka, kb = lay["krA"] // LANE, lay["krB"] // LANE
    k_cat = pl.pallas_call(
        functools.partial(_k_up_body, hpt=hpt),
        grid=(t // tm, heads // hpt),
        in_specs=[pl.BlockSpec((tm, kvr), lambda i, j: (i, 0)),
                  pl.BlockSpec((kvr, hpt * LANE), lambda i, j: (0, j)),
                  pl.BlockSpec((tm, LANE), lambda i, j: (i, ka)),
                  pl.BlockSpec((tm, LANE), lambda i, j: (i, kb)),
                  pl.BlockSpec((tm, LANE), lambda i, j: (i % nrb, 0)),
                  pl.BlockSpec((tm, LANE), lambda i, j: (i % nrb, 0))],
        out_specs=pl.BlockSpec((tm, hpt * 2 * LANE), lambda i, j: (i, j)),
        out_shape=jax.ShapeDtypeStruct((t, heads * 2 * LANE), BF16),
        compiler_params=_cp(("parallel", "arbitrary")),
        name="mla_k_up",
    )(ckvn, wk, proj, proj, ctab, stab)
    v_ext = pl.pallas_call(
        functools.partial(_v_up_body, hpt=hpt),
        grid=(t // tm, heads // hpt),
        in_specs=[pl.BlockSpec((tm, kvr), lambda i, j: (i, 0)),
                  pl.BlockSpec((kvr, hpt * LANE), lambda i, j: (0, j))],
        out_specs=pl.BlockSpec((tm, hpt * 2 * LANE), lambda i, j: (i, j)),
        out_shape=jax.ShapeDtypeStruct((t, heads * 2 * LANE), BF16),
        compiler_params=_cp(("parallel", "arbitrary")),
        name="mla_v_up",
    )(ckvn, wv)
    return mla_attention(q_cat, k_cat, v_ext, batch, seq)


def _mla_attn_body(q_ref, k_ref, v_ref, o_ref, m_ref, acc_ref, *, tq, hg):
    qi = pl.program_id(2)
    m_ref[...] = jnp.full_like(m_ref, NEG_BIG)
    acc_ref[...] = jnp.zeros_like(acc_ref)
    w = 2 * LANE

    def block(ki, masked):
        ks = pl.multiple_of(ki * tq, tq)
        for g in range(hg):
            s = _dot_nt(q_ref[:, g * w:(g + 1) * w], k_ref[pl.ds(ks, tq), g * w:(g + 1) * w])
            if masked:
                qc = lax.broadcasted_iota(I32, (tq, tq), 0) // CHUNK
                kc = lax.broadcasted_iota(I32, (tq, tq), 1) // CHUNK
                s = jnp.where(kc <= qc, s, NEG_BIG)
            m_prev = m_ref[g]
            m_new = jnp.maximum(m_prev, jnp.max(s, axis=-1, keepdims=True))
            p = jnp.exp2(s - m_new).astype(BF16)
            acc_ref[g] = (jnp.exp2(m_prev - m_new) * acc_ref[g]
                          + _dot(p, v_ref[pl.ds(ks, tq), g * w:(g + 1) * w]))
            m_ref[g] = m_new

    def body(ki, carry):
        block(ki, False)
        return carry

    lax.fori_loop(0, qi, body, 0)
    block(qi, True)
    for g in range(hg):
        acc = acc_ref[g]
        o_ref[:, g * LANE:(g + 1) * LANE] = (acc[:, :LANE] / acc[:, LANE:LANE + 1]).astype(o_ref.dtype)


def mla_attention(q_cat, k_cat, v_ext, batch, seq):
    heads = MLA_HEADS
    t = batch * seq
    tq = _tile(seq, 512)
    nq = seq // tq
    hg = 4 if heads % 4 == 0 else 1
    w = 2 * LANE
    return pl.pallas_call(
        functools.partial(_mla_attn_body, tq=tq, hg=hg),
        grid=(batch, heads // hg, nq),
        in_specs=[pl.BlockSpec((tq, hg * w), lambda b, h, i: (b * nq + i, h)),
                  pl.BlockSpec((seq, hg * w), lambda b, h, i: (b, h)),
                  pl.BlockSpec((seq, hg * w), lambda b, h, i: (b, h))],
        out_specs=pl.BlockSpec((tq, hg * LANE), lambda b, h, i: (b * nq + i, h)),
        out_shape=jax.ShapeDtypeStruct((t, heads * MLA_V), BF16),
        scratch_shapes=[pltpu.VMEM((hg, tq, 1), F32), pltpu.VMEM((hg, tq, w), F32)],
        compiler_params=_cp(("parallel", "parallel", "arbitrary")),
        name="mla_attention",
    )(q_cat, k_cat, v_ext)


def _merge_body(a1_ref, w1_ref, a2_ref, w2_ref, g1_ref, g2_ref, o_ref):
    y1 = _dot(a1_ref[...], w1_ref[...])
    y2 = _dot(a2_ref[...], w2_ref[...])
    o_ref[...] = (jax.nn.sigmoid(g1_ref[...].astype(F32)) * y1
                  + jax.nn.sigmoid(g2_ref[...].astype(F32)) * y2).astype(o_ref.dtype)


def merge_branches(o_gla, o_mla, w_gla, w_mla, proj, lay, d_model):
    t = o_gla.shape[0]
    tm = _tile(t, 1024)
    tn = _tile(lay["gate_tile"], 512)
    g1, g2 = lay["gate_gla"] // tn, lay["gate_mla"] // tn
    k1, k2 = o_gla.shape[1], o_mla.shape[1]
    return pl.pallas_call(
        _merge_body,
        grid=(t // tm, d_model // tn),
        in_specs=[pl.BlockSpec((tm, k1), lambda i, j: (i, 0)),
                  pl.BlockSpec((k1, tn), lambda i, j: (0, j)),
                  pl.BlockSpec((tm, k2), lambda i, j: (i, 0)),
                  pl.BlockSpec((k2, tn), lambda i, j: (0, j)),
                  pl.BlockSpec((tm, tn), lambda i, j: (i, g1 + j)),
                  pl.BlockSpec((tm, tn), lambda i, j: (i, g2 + j))],
        out_specs=pl.BlockSpec((tm, tn), lambda i, j: (i, j)),
        out_shape=jax.ShapeDtypeStruct((t, d_model), BF16),
        compiler_params=_cp(("parallel", "arbitrary")),
        name="merge_branches",
    )(o_gla, w_gla, o_mla, w_mla, proj, proj)


def _mm_res_body(a_ref, w_ref, r_ref, o_ref):
    o_ref[...] = r_ref[...] + _dot(a_ref[...], w_ref[...])


def matmul_residual(a, w, res, *, tm=1024, tn=512, name="matmul_residual"):
    m, k = a.shape
    n = w.shape[1]
    tm, tn = _tile(m, tm), _tile(n, tn)
    return pl.pallas_call(
        _mm_res_body,
        grid=(m // tm, n // tn),
        in_specs=[pl.BlockSpec((tm, k), lambda i, j: (i, 0)),
                  pl.BlockSpec((k, tn), lambda i, j: (0, j)),
                  pl.BlockSpec((tm, tn), lambda i, j: (i, j))],
        out_specs=pl.BlockSpec((tm, tn), lambda i, j: (i, j)),
        out_shape=jax.ShapeDtypeStruct((m, n), F32),
        compiler_params=_cp(("parallel", "arbitrary")),
        name=name,
    )(a, w, res)


def _pack_bf16_pairs(x):
    n = x.shape[1] // 2
    bits = lax.bitcast_convert_type(x.astype(BF16).astype(F32), U32)
    return (bits[:, :n] >> 16) | (bits[:, n:] & jnp.uint32(0xFFFF0000))


def _unpack_bf16_pairs(w, out_ref):
    n = w.shape[1]
    out_ref[:, :n] = lax.bitcast_convert_type(w << 16, F32).astype(BF16)
    out_ref[:, n:] = lax.bitcast_convert_type(w & jnp.uint32(0xFFFF0000), F32).astype(BF16)


def _router_body(h_ref, g_ref, rw_ref, rb_ref, tri_ref, xp_ref, idx_ref, rank_ref, cw_ref, cnt_ref, carry_ref):
    @pl.when(pl.program_id(0) == 0)
    def _():
        carry_ref[...] = jnp.zeros_like(carry_ref)

    hn = _rms(h_ref[...], g_ref[...])
    xp_ref[...] = _pack_bf16_pairs(hn)
    logits = jnp.dot(hn, rw_ref[...], precision=lax.Precision.HIGHEST, preferred_element_type=F32) + rb_ref[...]
    tm = logits.shape[0]
    lane = lax.broadcasted_iota(I32, (tm, LANE), 1)
    work = logits
    vals, sels, idxs = [], [], []
    for _ in range(TOP_K):
        mx = jnp.max(work, axis=-1, keepdims=True)
        ix = jnp.min(jnp.where(work == mx, lane, LANE), axis=-1, keepdims=True)
        sel = lane == ix
        work = jnp.where(sel, -jnp.inf, work)
        vals.append(mx)
        sels.append(sel)
        idxs.append(ix)
    es = [jnp.exp(v - vals[0]) for v in vals]
    den = es[0]
    for e in es[1:]:
        den = den + e
    hot = jnp.zeros((tm, LANE), F32)
    for sel in sels:
        hot = hot + sel.astype(F32)
    carry = carry_ref[0:1, :]
    rank = _dot(tri_ref[...], hot.astype(BF16)) + carry
    idx_out = jnp.zeros((tm, LANE), I32)
    rank_out = jnp.zeros((tm, LANE), F32)
    cw_out = jnp.zeros((tm, LANE), F32)
    for k in range(TOP_K):
        rk = jnp.sum(jnp.where(sels[k], rank, 0.0), axis=-1, keepdims=True)
        idx_out = jnp.where(lane == k, idxs[k], idx_out)
        rank_out = jnp.where(lane == k, rk, rank_out)
        cw_out = jnp.where(lane == k, es[k] / den, cw_out)
    idx_ref[...] = idx_out
    rank_ref[...] = rank_out.astype(I32)
    cw_ref[...] = cw_out
    new_carry = carry + jnp.sum(hot, axis=0, keepdims=True)
    carry_ref[...] = jnp.broadcast_to(new_carry, carry_ref.shape)
    cnt_ref[...] = jnp.broadcast_to(new_carry, cnt_ref.shape)


def moe_router(h, norm_g, router_w, router_b, tm):
    t, d = h.shape
    ne = router_w.shape[1]
    assert ne <= LANE and d % (2 * LANE) == 0
    rw = jnp.zeros((d, LANE), F32).at[:, :ne].set(router_w)
    rb = jnp.full((1, LANE), NEG_BIG, F32).at[0, :ne].set(router_b)
    r = jnp.arange(tm)
    tri = (r[None, :] < r[:, None]).astype(BF16)
    row = lambda i: (i, 0)
    const = lambda i: (0, 0)
    return pl.pallas_call(
        _router_body,
        grid=(t // tm,),
        in_specs=[pl.BlockSpec((tm, d), row), pl.BlockSpec((1, d), const),
                  pl.BlockSpec((d, LANE), const), pl.BlockSpec((1, LANE), const),
                  pl.BlockSpec((tm, tm), const)],
        out_specs=[pl.BlockSpec((tm, d // 2), row), pl.BlockSpec((tm, LANE), row),
                   pl.BlockSpec((tm, LANE), row), pl.BlockSpec((tm, LANE), row),
                   pl.BlockSpec((8, LANE), const)],
        out_shape=[jax.ShapeDtypeStruct((t, d // 2), U32), jax.ShapeDtypeStruct((t, LANE), I32),
                   jax.ShapeDtypeStruct((t, LANE), I32), jax.ShapeDtypeStruct((t, LANE), F32),
                   jax.ShapeDtypeStruct((8, LANE), F32)],
        scratch_shapes=[pltpu.VMEM((8, LANE), F32)],
        compiler_params=_cp(("arbitrary",)),
        name="moe_router",
    )(h, norm_g.reshape(1, -1).astype(F32), rw, rb, tri)


def _scatter_body(pos_ref, x_ref, xs_in_ref, xs_ref, sem, *, tm):
    del xs_in_ref
    base = pl.program_id(0) * (tm * TOP_K)

    def issue(r, carry):
        for k in range(TOP_K):
            p = pos_ref[base + r * TOP_K + k]
            pltpu.make_async_copy(x_ref.at[pl.ds(r, 1)], xs_ref.at[pl.ds(p, 1)], sem).start()
        return carry

    lax.fori_loop(0, tm, issue, 0)
    for _ in range(TOP_K):
        pltpu.make_async_copy(x_ref, xs_ref.at[pl.ds(0, tm)], sem).wait()


def moe_scatter(xp, pos_flat, n_rows, tm):
    t, w = xp.shape
    grid_spec = pltpu.PrefetchScalarGridSpec(
        num_scalar_prefetch=1,
        grid=(t // tm,),
        in_specs=[pl.BlockSpec((tm, w), lambda i, pos: (i, 0)),
                  pl.BlockSpec(memory_space=pl.ANY)],
        out_specs=pl.BlockSpec(memory_space=pl.ANY),
        scratch_shapes=[pltpu.SemaphoreType.DMA(())],
    )
    return pl.pallas_call(
        functools.partial(_scatter_body, tm=tm),
        grid_spec=grid_spec,
        out_shape=jax.ShapeDtypeStruct((n_rows, w), xp.dtype),
        input_output_aliases={2: 0},
        compiler_params=_cp(("arbitrary",)),
        name="moe_scatter",
    )(pos_flat, xp, jnp.zeros((n_rows, w), xp.dtype))


def _moe_up_body(te_ref, nu_ref, xs_ref, w_ref, b_ref, o_ref, xb_ref, *, de):
    i = pl.program_id(0)

    @pl.when(i < nu_ref[0])
    def _():
        _unpack_bf16_pairs(xs_ref[...], xb_ref)
        gu = _dot(xb_ref[...], w_ref[...]) + b_ref[...]
        gate = jnp.minimum(gu[:, :de], SWIGLU_LIMIT)
        up = jnp.clip(gu[:, de:], -SWIGLU_LIMIT, SWIGLU_LIMIT)
        o_ref[...] = ((up + 1.0) * (gate * jax.nn.sigmoid(SWIGLU_ALPHA * gate))).astype(o_ref.dtype)

    @pl.when(i >= nu_ref[0])
    def _():
        o_ref[...] = jnp.zeros_like(o_ref)


def _moe_down_body(te_ref, nu_ref, a_ref, w_ref, b_ref, o_ref):
    i = pl.program_id(0)

    @pl.when(i < nu_ref[0])
    def _():
        o_ref[...] = _dot(a_ref[...], w_ref[...]) + b_ref[...]

    @pl.when(i >= nu_ref[0])
    def _():
        o_ref[...] = jnp.zeros_like(o_ref)


def moe_experts(xs, tile_expert, n_used, w_gu, b_gu, w_dn, b_dn, tm):
    n_rows, half = xs.shape
    d = 2 * half
    ne, _, de2 = w_gu.shape
    de = de2 // 2
    n_tiles = n_rows // tm
    act = pl.pallas_call(
        functools.partial(_moe_up_body, de=de),
        grid_spec=pltpu.PrefetchScalarGridSpec(
            num_scalar_prefetch=2,
            grid=(n_tiles,),
            in_specs=[pl.BlockSpec((tm, half), lambda i, te, nu: (i, 0)),
                      pl.BlockSpec((None, d, de2), lambda i, te, nu: (te[i], 0, 0)),
                      pl.BlockSpec((None, 1, de2), lambda i, te, nu: (te[i], 0, 0))],
            out_specs=pl.BlockSpec((tm, de), lambda i, te, nu: (i, 0)),
            scratch_shapes=[pltpu.VMEM((tm, d), BF16)],
        ),
        out_shape=jax.ShapeDtypeStruct((n_rows, de), BF16),
        compiler_params=_cp(("arbitrary",)),
        name="moe_up",
    )(tile_expert, n_used, xs, w_gu, b_gu.reshape(ne, 1, de2))
    return pl.pallas_call(
        _moe_down_body,
        grid_spec=pltpu.PrefetchScalarGridSpec(
            num_scalar_prefetch=2,
            grid=(n_tiles,),
            in_specs=[pl.BlockSpec((tm, de), lambda i, te, nu: (i, 0)),
                      pl.BlockSpec((None, de, d), lambda i, te, nu: (te[i], 0, 0)),
                      pl.BlockSpec((None, 1, d), lambda i, te, nu: (te[i], 0, 0))],
            out_specs=pl.BlockSpec((tm, d), lambda i, te, nu: (i, 0)),
        ),
        out_shape=jax.ShapeDtypeStruct((n_rows, d), F32),
        compiler_params=_cp(("arbitrary",)),
        name="moe_down",
    )(tile_expert, n_used, act, w_dn, b_dn.reshape(ne, 1, d))


def _combine_body(pos_ref, h_ref, cw_ref, g_ref, ys_ref, h2_ref, hn_ref, buf_ref, sem, *, tm):
    i = pl.program_id(0)

    def issue(tile, slot):
        base = tile * (tm * TOP_K)

        def row(r, carry):
            for k in range(TOP_K):
                p = pos_ref[base + r * TOP_K + k]
                pltpu.make_async_copy(ys_ref.at[pl.ds(p, 1)], buf_ref.at[slot, k, pl.ds(r, 1)],
                                      sem.at[slot]).start()
            return carry

        lax.fori_loop(0, tm, row, 0)

    @pl.when(i == 0)
    def _():
        issue(0, 0)

    @pl.when(i + 1 < pl.num_programs(0))
    def _():
        issue(i + 1, (i + 1) % 2)

    slot = i % 2
    for k in range(TOP_K):
        pltpu.make_async_copy(ys_ref.at[pl.ds(0, tm)], buf_ref.at[slot, k], sem.at[slot]).wait()
    acc = h_ref[...]
    cw = cw_ref[...]
    for k in range(TOP_K):
        acc = acc + cw[:, k:k + 1] * buf_ref[slot, k]
    h2_ref[...] = acc
    hn_ref[...] = _rms(acc, g_ref[...]).astype(hn_ref.dtype)


def moe_combine(h, cw, ys, pos_flat, norm_g, tm):
    t, d = h.shape
    grid_spec = pltpu.PrefetchScalarGridSpec(
        num_scalar_prefetch=1,
        grid=(t // tm,),
        in_specs=[pl.BlockSpec((tm, d), lambda i, pos: (i, 0)),
                  pl.BlockSpec((tm, LANE), lambda i, pos: (i, 0)),
                  pl.BlockSpec((1, d), lambda i, pos: (0, 0)),
                  pl.BlockSpec(memory_space=pl.ANY)],
        out_specs=[pl.BlockSpec((tm, d), lambda i, pos: (i, 0)),
                   pl.BlockSpec((tm, d), lambda i, pos: (i, 0))],
        scratch_shapes=[pltpu.VMEM((2, TOP_K, tm, d), F32), pltpu.SemaphoreType.DMA((2,))],
    )
    return pl.pallas_call(
        functools.partial(_combine_body, tm=tm),
        grid_spec=grid_spec,
        out_shape=[jax.ShapeDtypeStruct((t, d), F32), jax.ShapeDtypeStruct((t, d), BF16)],
        compiler_params=_cp(("arbitrary",)),
        name="moe_combine",
    )(pos_flat, h, cw, norm_g.reshape(1, -1).astype(F32), ys)


def _ple_body(a_ref, w_ref, p_ref, wp_ref, h_ref, o_ref):
    gate = _dot(a_ref[...], w_ref[...])
    pp = _dot(p_ref[...].astype(BF16), wp_ref[...])
    o_ref[...] = h_ref[...] + jax.nn.sigmoid(gate) * pp


def ple_gate(hn, w_gate, p, w_proj, h, *, tm=1024, tn=512):
    t, d = h.shape
    pd = p.shape[1]
    tm, tn = _tile(t, tm), _tile(d, tn)
    return pl.pallas_call(
        _ple_body,
        grid=(t // tm, d // tn),
        in_specs=[pl.BlockSpec((tm, d), lambda i, j: (i, 0)),
                  pl.BlockSpec((d, tn), lambda i, j: (0, j)),
                  pl.BlockSpec((tm, pd), lambda i, j: (i, 0)),
                  pl.BlockSpec((pd, tn), lambda i, j: (0, j)),
                  pl.BlockSpec((tm, tn), lambda i, j: (i, j))],
        out_specs=pl.BlockSpec((tm, tn), lambda i, j: (i, j)),
        out_shape=jax.ShapeDtypeStruct((t, d), F32),
        compiler_params=_cp(("parallel", "arbitrary")),
        name="ple_gate",
    )(hn, w_gate, p, w_proj, h)


def _proj_layout(d_model, qkw, vw, qr, kvr):
    lay, off = {}, 0
    for name, width in (("gq", qkw), ("gk", qkw), ("gv", vw), ("gg", vw), ("cq", qr), ("ckv", kvr),
                        ("krA", LANE), ("krB", LANE), ("gf", LANE)):
        lay[name] = off
        off += width
    gate_tile = _tile(d_model, 1024)
    off = -(-off // gate_tile) * gate_tile
    lay["gate_gla"] = off
    lay["gate_mla"] = off + d_model
    lay["total"] = off + 2 * d_model
    lay["gate_tile"] = gate_tile
    return lay


def _reorder_w_in(w_in, lay, qkw, vw, rank, qr, kvr, d_model):
    splits, off = [], 0
    for width in (qkw, qkw, vw, vw, rank, qr, kvr, MLA_ROPE, d_model, d_model):
        splits.append(w_in[:, off:off + width])
        off += width
    gq, gk, gv, gg, gf, cq, ckv, kr, g_gla, g_mla = splits
    front = jnp.concatenate([gq, gk, gv, gg, cq, ckv, _pad_lane(kr), _pad_lane(_swap_halves(kr)),
                             _pad_lane(gf)], axis=1)
    front = jnp.pad(front, ((0, 0), (0, lay["gate_gla"] - front.shape[1])))
    return jnp.concatenate([front, g_gla, g_mla], axis=1).astype(BF16)


def _layer(h, p, w_in, gla_gate_up, gla_gate_bias, gla_out_norm, mla_q_norm, mla_kv_norm, mla_w_uq,
           mla_w_ukv, w_branch_gla, w_branch_mla, w_out, norm_mix, norm_moe, router_w, router_b,
           expert_w_gate_up, expert_b_gate_up, expert_w_down, expert_b_down, norm_ple, ple_gate_w,
           ple_proj_w, batch, seq):
    t, d = h.shape
    rank, qkw = gla_gate_up.shape
    vw = w_branch_gla.shape[0]
    qr, kvr = mla_q_norm.shape[0], mla_kv_norm.shape[0]
    lay = _proj_layout(d, qkw, vw, qr, kvr)

    xn = rms_norm(h, norm_mix)
    proj = matmul(xn, _reorder_w_in(w_in, lay, qkw, vw, rank, qr, kvr, d), out_dtype=BF16,
                  tn=lay["gate_tile"], name="in_proj")
    o_gla = gla_mixer(proj, lay, gla_gate_up, gla_gate_bias, gla_out_norm, batch, seq)
    o_mla = mla_mixer(proj, lay, mla_q_norm, mla_kv_norm, mla_w_uq, mla_w_ukv, batch, seq)
    merged = merge_branches(o_gla, o_mla, w_branch_gla.astype(BF16), w_branch_mla.astype(BF16), proj, lay, d)
    h1 = matmul_residual(merged, w_out.astype(BF16), h, name="out_proj")

    ne = router_w.shape[1]
    tm_e = _tile(t, 256)
    xp, idx, rank_in, cw, cnt = moe_router(h1, norm_moe, router_w, router_b, tm_e)
    counts = cnt[0, :ne].astype(I32)
    padded = -(-counts // tm_e) * tm_e
    gend = jnp.cumsum(padded)
    gstart = gend - padded
    pos = (gstart[idx[:, :TOP_K]] + rank_in[:, :TOP_K]).reshape(-1)
    n_tiles = t * TOP_K // tm_e + ne
    tile_start = jnp.arange(n_tiles, dtype=I32) * tm_e
    tile_expert = jnp.minimum(jnp.sum((gend[None, :] <= tile_start[:, None]).astype(I32), axis=1), ne - 1)
    n_used = (gend[-1:] // tm_e).astype(I32)
    xs = moe_scatter(xp, pos, n_tiles * tm_e, tm_e)
    ys = moe_experts(xs, tile_expert, n_used, expert_w_gate_up.astype(BF16), expert_b_gate_up,
                     expert_w_down.astype(BF16), expert_b_down, tm_e)
    h2, hn2 = moe_combine(h1, cw, ys, pos, norm_ple, _tile(t, 128))

    return ple_gate(hn2, ple_gate_w.astype(BF16), p, ple_proj_w.astype(BF16), h2)


def kernel(x, p, w_in, gla_gate_up, gla_gate_bias, gla_out_norm, mla_q_norm, mla_kv_norm, mla_w_uq, mla_w_ukv, w_branch_gla, w_branch_mla, w_out, norm_mix, norm_moe, router_w, router_b, expert_w_gate_up, expert_b_gate_up, expert_w_down, expert_b_down, norm_ple, ple_gate_w, ple_proj_w, final_norm):
    batch, seq, d = x.shape
    h = x.reshape(batch * seq, d)
    for i in range(p.shape[0]):
        h = _layer(h, p[i].reshape(batch * seq, -1), w_in[i], gla_gate_up[i], gla_gate_bias[i],
                   gla_out_norm[i], mla_q_norm[i], mla_kv_norm[i], mla_w_uq[i], mla_w_ukv[i],
                   w_branch_gla[i], w_branch_mla[i], w_out[i], norm_mix[i], norm_moe[i], router_w[i],
                   router_b[i], expert_w_gate_up[i], expert_b_gate_up[i], expert_w_down[i],
                   expert_b_down[i], norm_ple[i], ple_gate_w[i], ple_proj_w[i], batch, seq)
    return rms_norm(h, final_norm, out_dtype=F32).reshape(batch, seq, d)
```

```python
import functools
import math

import jax
import jax.numpy as jnp
from jax import lax
from jax.experimental import pallas as pl
from jax.experimental.pallas import tpu as pltpu

F32 = jnp.float32
BF16 = jnp.bfloat16
U32 = jnp.uint32
I32 = jnp.int32

CHUNK = 64
EPS = 1e-6
GLA_HEADS = 8
GLA_TAU = 16.0
MLA_HEADS = 16
MLA_NOPE = 128
MLA_ROPE = 64
MLA_V = 128
ROPE_THETA = 10000.0
TOP_K = 4
SWIGLU_LIMIT = 7.0
SWIGLU_ALPHA = 1.702

LANE = 128
SUBLANE = 8
VMEM_LIMIT = 56 * 2 ** 20
NEG_BIG = -1e30


def _tile(n, pref):
    if n <= pref:
        return n
    t = pref
    while t >= 8:
        if n % t == 0 and t % 8 == 0:
            return t
        t -= 8
    return n


def _cp(sem):
    return pltpu.CompilerParams(dimension_semantics=sem, vmem_limit_bytes=VMEM_LIMIT)


def _dot(a, b):
    return jnp.dot(a, b, preferred_element_type=F32)


def _dot_nt(a, b):
    return lax.dot_general(a, b, (((1,), (1,)), ((), ())), preferred_element_type=F32)


def _dot_tn(a, b):
    return lax.dot_general(a, b, (((0,), (0,)), ((), ())), preferred_element_type=F32)


def _rms(x, g):
    return x * lax.rsqrt(jnp.mean(x * x, axis=-1, keepdims=True) + EPS) * g


def _rms_body(x_ref, g_ref, o_ref):
    o_ref[...] = _rms(x_ref[...].astype(F32), g_ref[...]).astype(o_ref.dtype)


def rms_norm(x, g, *, col_block=0, width=None, out_dtype=BF16, tm=512):
    m = x.shape[0]
    width = width or x.shape[1]
    tm = _tile(m, tm)
    return pl.pallas_call(
        _rms_body,
        grid=(m // tm,),
        in_specs=[pl.BlockSpec((tm, width), lambda i: (i, col_block)),
                  pl.BlockSpec((1, width), lambda i: (0, 0))],
        out_specs=pl.BlockSpec((tm, width), lambda i: (i, 0)),
        out_shape=jax.ShapeDtypeStruct((m, width), out_dtype),
        compiler_params=_cp(("parallel",)),
        name="rms_norm",
    )(x, g.reshape(1, -1).astype(F32))


def _mm_body(a_ref, w_ref, o_ref):
    o_ref[...] = _dot(a_ref[...], w_ref[...]).astype(o_ref.dtype)


def matmul(a, w, *, out_dtype, tm=1024, tn=1024, name="matmul"):
    m, k = a.shape
    n = w.shape[1]
    tm, tn = _tile(m, tm), _tile(n, tn)
    return pl.pallas_call(
        _mm_body,
        grid=(m // tm, n // tn),
        in_specs=[pl.BlockSpec((tm, k), lambda i, j: (i, 0)),
                  pl.BlockSpec((k, tn), lambda i, j: (0, j))],
        out_specs=pl.BlockSpec((tm, tn), lambda i, j: (i, j)),
        out_shape=jax.ShapeDtypeStruct((m, n), out_dtype),
        compiler_params=_cp(("parallel", "arbitrary")),
        name=name,
    )(a, w)


def _gla_body(q_ref, k_ref, v_ref, gg_ref, gf_ref, gup_ref, gb_ref, on_ref, tri_ref, blk_ref,
              o_ref, st_ref, ob_ref, *, nchunk, scale):
    @pl.when(pl.program_id(2) == 0)
    def _():
        st_ref[...] = jnp.zeros_like(st_ref)

    logit = _dot(gf_ref[...], gup_ref[...]) + gb_ref[...]
    la = (jnp.minimum(logit, 0.0) - jnp.log(1.0 + jnp.exp(-jnp.abs(logit)))) * (1.0 / GLA_TAU)
    la_hi = la.astype(BF16)
    la_lo = (la - la_hi.astype(F32)).astype(BF16)
    tri = tri_ref[...]
    blk = blk_ref[...]
    b = _dot(tri, la_hi) + _dot(tri, la_lo)
    bl = _dot(blk, la_hi) + _dot(blk, la_lo)
    q = q_ref[...].astype(F32)
    k = k_ref[...].astype(F32)
    qd = (q * scale * jnp.exp(b)).astype(BF16)
    kd = (k * jnp.exp(-b)).astype(BF16)
    ke = (k * jnp.exp(bl - b)).astype(BF16)
    dec = jnp.exp(bl)
    row = lax.broadcasted_iota(I32, (CHUNK, CHUNK), 0)
    col = lax.broadcasted_iota(I32, (CHUNK, CHUNK), 1)
    causal = row >= col
    for c in range(nchunk):
        lo = c * CHUNK
        qc = qd[lo:lo + CHUNK]
        kc = kd[lo:lo + CHUNK]
        kec = ke[lo:lo + CHUNK]
        vc = v_ref[lo:lo + CHUNK, :]
        s = jnp.where(causal, _dot_nt(qc, kc), 0.0).astype(BF16)
        st = st_ref[...]
        ob_ref[lo:lo + CHUNK, :] = _dot(s, vc) + _dot_nt(qc, st.astype(BF16))
        st_ref[...] = st * dec[lo:lo + 1, :] + _dot_tn(vc, kec)
    o = ob_ref[...]
    o = _rms(o, on_ref[...])
    g = gg_ref[...].astype(F32)
    o_ref[...] = (o * (g * jax.nn.sigmoid(g))).astype(o_ref.dtype)


def gla_mixer(proj, lay, gate_up, gate_bias, out_norm, batch, seq):
    heads = GLA_HEADS
    dk = gate_up.shape[1] // heads
    dv = out_norm.shape[0]
    rank = gate_up.shape[0]
    lb = _tile(seq, 512)
    nchunk = lb // CHUNK
    nsb = seq // lb
    t = batch * seq
    gup = jnp.zeros((LANE, heads * dk), F32).at[:rank].set(gate_up).astype(BF16)
    r = jnp.arange(lb)
    same = (r[:, None] // CHUNK) == (r[None, :] // CHUNK)
    tri = (same & (r[None, :] <= r[:, None])).astype(BF16)
    blk = same.astype(BF16)
    qb, kb = lay["gq"] // dk, lay["gk"] // dk
    vb, gb = lay["gv"] // dv, lay["gg"] // dv
    fb = lay["gf"] // LANE
    rowmap = lambda b, h, s: b * nsb + s
    return pl.pallas_call(
        functools.partial(_gla_body, nchunk=nchunk, scale=dk ** -0.5),
        grid=(batch, heads, nsb),
        in_specs=[
            pl.BlockSpec((lb, dk), lambda b, h, s: (rowmap(b, h, s), qb + h)),
            pl.BlockSpec((lb, dk), lambda b, h, s: (rowmap(b, h, s), kb + h)),
            pl.BlockSpec((lb, dv), lambda b, h, s: (rowmap(b, h, s), vb + h)),
            pl.BlockSpec((lb, dv), lambda b, h, s: (rowmap(b, h, s), gb + h)),
            pl.BlockSpec((lb, LANE), lambda b, h, s: (rowmap(b, h, s), fb)),
            pl.BlockSpec((LANE, dk), lambda b, h, s: (0, h)),
            pl.BlockSpec((1, dk), lambda b, h, s: (0, h)),
            pl.BlockSpec((1, dv), lambda b, h, s: (0, 0)),
            pl.BlockSpec((lb, lb), lambda b, h, s: (0, 0)),
            pl.BlockSpec((lb, lb), lambda b, h, s: (0, 0)),
        ],
        out_specs=pl.BlockSpec((lb, dv), lambda b, h, s: (rowmap(b, h, s), h)),
        out_shape=jax.ShapeDtypeStruct((t, heads * dv), BF16),
        scratch_shapes=[pltpu.VMEM((dv, dk), F32), pltpu.VMEM((lb, dv), F32)],
        compiler_params=_cp(("parallel", "parallel", "arbitrary")),
        name="gla_mixer",
    )(proj, proj, proj, proj, proj, gup, gate_bias.reshape(1, -1).astype(F32),
      out_norm.reshape(1, -1).astype(F32), tri, blk)


def _q_up_body(a_ref, w_ref, c_ref, s_ref, o_ref, *, hpt, scale):
    acc = _dot(a_ref[...], w_ref[...])
    c = c_ref[...]
    s = s_ref[...]
    for h in range(hpt):
        src = h * 3 * LANE
        dst = h * 2 * LANE
        o_ref[:, dst:dst + LANE] = (acc[:, src:src + LANE] * scale).astype(o_ref.dtype)
        pe = acc[:, src + LANE:src + 2 * LANE] * c + acc[:, src + 2 * LANE:src + 3 * LANE] * s
        o_ref[:, dst + LANE:dst + 2 * LANE] = (pe * scale).astype(o_ref.dtype)


def _k_up_body(a_ref, w_ref, kra_ref, krb_ref, c_ref, s_ref, o_ref, *, hpt):
    acc = _dot(a_ref[...], w_ref[...])
    kpe = (kra_ref[...].astype(F32) * c_ref[...] + krb_ref[...].astype(F32) * s_ref[...]).astype(o_ref.dtype)
    for h in range(hpt):
        o_ref[:, 2 * h * LANE:(2 * h + 1) * LANE] = acc[:, h * LANE:(h + 1) * LANE].astype(o_ref.dtype)
        o_ref[:, (2 * h + 1) * LANE:(2 * h + 2) * LANE] = kpe


def _v_up_body(a_ref, w_ref, o_ref, *, hpt):
    acc = _dot(a_ref[...], w_ref[...])
    lane = lax.broadcasted_iota(I32, (acc.shape[0], LANE), 1)
    ones_col = (lane == 0).astype(o_ref.dtype)
    for h in range(hpt):
        o_ref[:, 2 * h * LANE:(2 * h + 1) * LANE] = acc[:, h * LANE:(h + 1) * LANE].astype(o_ref.dtype)
        o_ref[:, (2 * h + 1) * LANE:(2 * h + 2) * LANE] = ones_col


def _swap_halves(w):
    half = w.shape[-1] // 2
    return jnp.concatenate([w[..., half:], w[..., :half]], axis=-1)


def _pad_lane(w):
    pad = LANE - w.shape[-1]
    return jnp.pad(w, [(0, 0)] * (w.ndim - 1) + [(0, pad)])


def mla_mixer(proj, lay, q_norm, kv_norm, w_uq, w_ukv, batch, seq):
    heads = MLA_HEADS
    t = batch * seq
    qr, kvr = q_norm.shape[0], kv_norm.shape[0]
    assert MLA_NOPE == LANE and MLA_V == LANE and MLA_ROPE <= LANE
    scale = (MLA_NOPE + MLA_ROPE) ** -0.5 * math.log2(math.e)
    tm = _tile(seq, 1024)
    nrb = seq // tm
    hpt = 4 if heads % 4 == 0 else 1

    pos = jnp.arange(seq, dtype=F32)
    inv_freq = ROPE_THETA ** (-jnp.arange(0, MLA_ROPE, 2, dtype=F32) / MLA_ROPE)
    ang = pos[:, None] * inv_freq[None, :]
    cos, sin = jnp.cos(ang), jnp.sin(ang)
    ctab = _pad_lane(jnp.concatenate([cos, cos], axis=-1))
    stab = _pad_lane(jnp.concatenate([-sin, sin], axis=-1))

    cqn = rms_norm(proj, q_norm, col_block=lay["cq"] // qr, width=qr)
    ckvn = rms_norm(proj, kv_norm, col_block=lay["ckv"] // kvr, width=kvr)

    wq = w_uq.reshape(qr, heads, MLA_NOPE + MLA_ROPE)
    wq_pe = wq[..., MLA_NOPE:]
    wq_r = jnp.concatenate([wq[..., :MLA_NOPE], _pad_lane(wq_pe), _pad_lane(_swap_halves(wq_pe))],
                           axis=-1).reshape(qr, heads * 3 * LANE).astype(BF16)
    q_cat = pl.pallas_call(
        functools.partial(_q_up_body, hpt=hpt, scale=scale),
        grid=(t // tm, heads // hpt),
        in_specs=[pl.BlockSpec((tm, qr), lambda i, j: (i, 0)),
                  pl.BlockSpec((qr, hpt * 3 * LANE), lambda i, j: (0, j)),
                  pl.BlockSpec((tm, LANE), lambda i, j: (i % nrb, 0)),
                  pl.BlockSpec((tm, LANE), lambda i, j: (i % nrb, 0))],
        out_specs=pl.BlockSpec((tm, hpt * 2 * LANE), lambda i, j: (i, j)),
        out_shape=jax.ShapeDtypeStruct((t, heads * 2 * LANE), BF16),
        compiler_params=_cp(("parallel", "arbitrary")),
        name="mla_q_up",
    )(cqn, wq_r, ctab, stab)

    wkv = w_ukv.reshape(kvr, heads, MLA_NOPE + MLA_V)
    wk = wkv[..., :MLA_NOPE].reshape(kvr, heads * MLA_NOPE).astype(BF16)
    wv = wkv[..., MLA_NOPE:].reshape(kvr, heads * MLA_V).astype(BF16)
    ka, kb = lay["krA"] // LANE, lay["krB"] // LANE
    k_cat = pl.pallas_call(
        functools.partial(_k_up_body, hpt=hpt),
        grid=(t // tm, heads // hpt),
        in_specs=[pl.BlockSpec((tm, kvr), lambda i, j: (i, 0)),
                  pl.BlockSpec((kvr, hpt * LANE), lambda i, j: (0, j)),
                  pl.BlockSpec((tm, LANE), lambda i, j: (i, ka)),
                  pl.BlockSpec((tm, LANE), lambda i, j: (i, kb)),
                  pl.BlockSpec((tm, LANE), lambda i, j: (i % nrb, 0)),
                  pl.BlockSpec((tm, LANE), lambda i, j: (i % nrb, 0))],
        out_specs=pl.BlockSpec((tm, hpt * 2 * LANE), lambda i, j: (i, j)),
        out_shape=jax.ShapeDtypeStruct((t, heads * 2 * LANE), BF16),
        compiler_params=_cp(("parallel", "arbitrary")),
        name="mla_k_up",
    )(ckvn, wk, proj, proj, ctab, stab)
    v_ext = pl.pallas_call(
        functools.partial(_v_up_body, hpt=hpt),
        grid=(t // tm, heads // hpt),
        in_specs=[pl.BlockSpec((tm, kvr), lambda i, j: (i, 0)),
                  pl.BlockSpec((kvr, hpt * LANE), lambda i, j: (0, j))],
        out_specs=pl.BlockSpec((tm, hpt * 2 * LANE), lambda i, j: (i, j)),
        out_shape=jax.ShapeDtypeStruct((t, heads * 2 * LANE), BF16),
        compiler_params=_cp(("parallel", "arbitrary")),
        name="mla_v_up",
    )(ckvn, wv)
    return mla_attention(q_cat, k_cat, v_ext, batch, seq)


def _mla_attn_body(q_ref, k_ref, v_ref, o_ref, *scratch, tq, hg):
    m_refs, a_refs, p_refs, acc_refs = (scratch[n * hg:(n + 1) * hg] for n in range(4))
    qi = pl.program_id(2)
    w = 2 * LANE
    for g in range(hg):
        m_refs[g][...] = jnp.full_like(m_refs[g], NEG_BIG)
        acc_refs[g][...] = jnp.zeros_like(acc_refs[g])

    def rows(ki):
        return pl.ds(pl.multiple_of(ki * tq, tq), tq)

    def scores(g, ki, masked):
        s = _dot_nt(q_ref[:, g * w:(g + 1) * w], k_ref[rows(ki), g * w:(g + 1) * w])
        if masked:
            qc = lax.broadcasted_iota(I32, (tq, tq), 0) // CHUNK
            kc = lax.broadcasted_iota(I32, (tq, tq), 1) // CHUNK
            s = jnp.where(kc <= qc, s, NEG_BIG)
        m_prev = m_refs[g][...]
        m_new = jnp.maximum(m_prev, jnp.max(s, axis=-1, keepdims=True))
        a_refs[g][...] = jnp.exp2(m_prev - m_new)
        p_refs[g][...] = jnp.exp2(s - m_new).astype(BF16)
        m_refs[g][...] = m_new

    def accumulate(g, ki):
        acc_refs[g][...] = (a_refs[g][...] * acc_refs[g][...]
                            + _dot(p_refs[g][...], v_ref[rows(ki), g * w:(g + 1) * w]))

    def fused(ki, masked):
        for g in range(hg):
            accumulate(g, ki - 1)
            scores(g, ki, masked)

    @pl.when(qi > 0)
    def _():
        for g in range(hg):
            scores(g, 0, False)

        def body(ki, carry):
            fused(ki, False)
            return carry

        lax.fori_loop(1, qi, body, 0)
        fused(qi, True)

    @pl.when(qi == 0)
    def _():
        for g in range(hg):
            scores(g, 0, True)

    for g in range(hg):
        accumulate(g, qi)
        acc = acc_refs[g][...]
        o_ref[:, g * LANE:(g + 1) * LANE] = (acc[:, :LANE] / acc[:, LANE:LANE + 1]).astype(o_ref.dtype)


def mla_attention(q_cat, k_cat, v_ext, batch, seq):
    heads = MLA_HEADS
    t = batch * seq
    tq = _tile(seq, 512)
    nq = seq // tq
    hg = 4 if heads % 4 == 0 else 1
    w = 2 * LANE
    return pl.pallas_call(
        functools.partial(_mla_attn_body, tq=tq, hg=hg),
        grid=(batch, heads // hg, nq),
        in_specs=[pl.BlockSpec((tq, hg * w), lambda b, h, i: (b * nq + i, h)),
                  pl.BlockSpec((seq, hg * w), lambda b, h, i: (b, h)),
                  pl.BlockSpec((seq, hg * w), lambda b, h, i: (b, h))],
        out_specs=pl.BlockSpec((tq, hg * LANE), lambda b, h, i: (b * nq + i, h)),
        out_shape=jax.ShapeDtypeStruct((t, heads * MLA_V), BF16),
        scratch_shapes=([pltpu.VMEM((tq, 1), F32)] * (2 * hg) + [pltpu.VMEM((tq, tq), BF16)] * hg
                        + [pltpu.VMEM((tq, w), F32)] * hg),
        compiler_params=_cp(("parallel", "parallel", "arbitrary")),
        name="mla_attention",
    )(q_cat, k_cat, v_ext)


def _merge_body(a1_ref, w1_ref, a2_ref, w2_ref, g1_ref, g2_ref, o_ref):
    y1 = _dot(a1_ref[...], w1_ref[...])
    y2 = _dot(a2_ref[...], w2_ref[...])
    o_ref[...] = (jax.nn.sigmoid(g1_ref[...].astype(F32)) * y1
                  + jax.nn.sigmoid(g2_ref[...].astype(F32)) * y2).astype(o_ref.dtype)


def merge_branches(o_gla, o_mla, w_gla, w_mla, proj, lay, d_model):
    t = o_gla.shape[0]
    tm = _tile(t, 1024)
    tn = _tile(lay["gate_tile"], 512)
    g1, g2 = lay["gate_gla"] // tn, lay["gate_mla"] // tn
    k1, k2 = o_gla.shape[1], o_mla.shape[1]
    return pl.pallas_call(
        _merge_body,
        grid=(t // tm, d_model // tn),
        in_specs=[pl.BlockSpec((tm, k1), lambda i, j: (i, 0)),
                  pl.BlockSpec((k1, tn), lambda i, j: (0, j)),
                  pl.BlockSpec((tm, k2), lambda i, j: (i, 0)),
                  pl.BlockSpec((k2, tn), lambda i, j: (0, j)),
                  pl.BlockSpec((tm, tn), lambda i, j: (i, g1 + j)),
                  pl.BlockSpec((tm, tn), lambda i, j: (i, g2 + j))],
        out_specs=pl.BlockSpec((tm, tn), lambda i, j: (i, j)),
        out_shape=jax.ShapeDtypeStruct((t, d_model), BF16),
        compiler_params=_cp(("parallel", "arbitrary")),
        name="merge_branches",
    )(o_gla, w_gla, o_mla, w_mla, proj, proj)


def _mm_res_body(a_ref, w_ref, r_ref, o_ref):
    o_ref[...] = r_ref[...] + _dot(a_ref[...], w_ref[...])


def matmul_residual(a, w, res, *, tm=1024, tn=512, name="matmul_residual"):
    m, k = a.shape
    n = w.shape[1]
    tm, tn = _tile(m, tm), _tile(n, tn)
    return pl.pallas_call(
        _mm_res_body,
        grid=(m // tm, n // tn),
        in_specs=[pl.BlockSpec((tm, k), lambda i, j: (i, 0)),
                  pl.BlockSpec((k, tn), lambda i, j: (0, j)),
                  pl.BlockSpec((tm, tn), lambda i, j: (i, j))],
        out_specs=pl.BlockSpec((tm, tn), lambda i, j: (i, j)),
        out_shape=jax.ShapeDtypeStruct((m, n), F32),
        compiler_params=_cp(("parallel", "arbitrary")),
        name=name,
    )(a, w, res)


def _router_body(h_ref, g_ref, rw_ref, rb_ref, tri_ref, hn_ref, idx_ref, rank_ref, cw_ref, cnt_ref, carry_ref):
    @pl.when(pl.program_id(0) == 0)
    def _():
        carry_ref[...] = jnp.zeros_like(carry_ref)

    hn = _rms(h_ref[...], g_ref[...])
    hn_ref[...] = hn
    logits = jnp.dot(hn, rw_ref[...], precision=lax.Precision.HIGHEST, preferred_element_type=F32) + rb_ref[...]
    tm = logits.shape[0]
    lane = lax.broadcasted_iota(I32, (tm, LANE), 1)
    work = logits
    vals, sels, idxs = [], [], []
    for _ in range(TOP_K):
        mx = jnp.max(work, axis=-1, keepdims=True)
        ix = jnp.min(jnp.where(work == mx, lane, LANE), axis=-1, keepdims=True)
        sel = lane == ix
        work = jnp.where(sel, -jnp.inf, work)
        vals.append(mx)
        sels.append(sel)
        idxs.append(ix)
    es = [jnp.exp(v - vals[0]) for v in vals]
    den = es[0]
    for e in es[1:]:
        den = den + e
    hot = jnp.zeros((tm, LANE), F32)
    for sel in sels:
        hot = hot + sel.astype(F32)
    carry = carry_ref[0:1, :]
    rank = _dot(tri_ref[...], hot.astype(BF16)) + carry
    idx_out = jnp.zeros((tm, LANE), I32)
    rank_out = jnp.zeros((tm, LANE), F32)
    cw_out = jnp.zeros((tm, LANE), F32)
    for k in range(TOP_K):
        rk = jnp.sum(jnp.where(sels[k], rank, 0.0), axis=-1, keepdims=True)
        idx_out = jnp.where(lane == k, idxs[k], idx_out)
        rank_out = jnp.where(lane == k, rk, rank_out)
        cw_out = jnp.where(lane == k, es[k] / den, cw_out)
    idx_ref[...] = idx_out
    rank_ref[...] = rank_out.astype(I32)
    cw_ref[...] = cw_out
    new_carry = carry + jnp.sum(hot, axis=0, keepdims=True)
    carry_ref[...] = jnp.broadcast_to(new_carry, carry_ref.shape)
    cnt_ref[...] = jnp.broadcast_to(new_carry, cnt_ref.shape)


def moe_router(h, norm_g, router_w, router_b, tm):
    t, d = h.shape
    ne = router_w.shape[1]
    assert ne <= LANE and d % (2 * LANE) == 0
    rw = jnp.zeros((d, LANE), F32).at[:, :ne].set(router_w)
    rb = jnp.full((1, LANE), NEG_BIG, F32).at[0, :ne].set(router_b)
    r = jnp.arange(tm)
    tri = (r[None, :] < r[:, None]).astype(BF16)
    row = lambda i: (i, 0)
    const = lambda i: (0, 0)
    return pl.pallas_call(
        _router_body,
        grid=(t // tm,),
        in_specs=[pl.BlockSpec((tm, d), row), pl.BlockSpec((1, d), const),
                  pl.BlockSpec((d, LANE), const), pl.BlockSpec((1, LANE), const),
                  pl.BlockSpec((tm, tm), const)],
        out_specs=[pl.BlockSpec((tm, d), row), pl.BlockSpec((tm, LANE), row),
                   pl.BlockSpec((tm, LANE), row), pl.BlockSpec((tm, LANE), row),
                   pl.BlockSpec((8, LANE), const)],
        out_shape=[jax.ShapeDtypeStruct((t, d), F32), jax.ShapeDtypeStruct((t, LANE), I32),
                   jax.ShapeDtypeStruct((t, LANE), I32), jax.ShapeDtypeStruct((t, LANE), F32),
                   jax.ShapeDtypeStruct((8, LANE), F32)],
        scratch_shapes=[pltpu.VMEM((8, LANE), F32)],
        compiler_params=_cp(("arbitrary",)),
        name="moe_router",
    )(h, norm_g.reshape(1, -1).astype(F32), rw, rb, tri)


def _sources_body(pos_ref, src_ref, *, nsteps):
    i = pl.program_id(0)
    rows_per = src_ref.shape[0] // nsteps
    toks_per = pos_ref.shape[0] // TOP_K // nsteps

    @pl.when(i < nsteps)
    def _():
        def clear(r, carry):
            src_ref[i * rows_per + r] = 0
            return carry

        lax.fori_loop(0, rows_per, clear, 0, unroll=8)

    @pl.when(i >= nsteps)
    def _():
        t0 = (i - nsteps) * toks_per

        def put(j, carry):
            for k in range(TOP_K):
                src_ref[pos_ref[(t0 + j) * TOP_K + k]] = t0 + j
            return carry

        lax.fori_loop(0, toks_per, put, 0, unroll=2)


def moe_sources(pos_flat, n_rows):
    n_tok = pos_flat.shape[0] // TOP_K
    nsteps = 64 if (n_rows % 64 == 0 and n_tok % 64 == 0) else 1
    return pl.pallas_call(
        functools.partial(_sources_body, nsteps=nsteps),
        grid=(2 * nsteps,),
        in_specs=[pl.BlockSpec(memory_space=pltpu.SMEM)],
        out_specs=pl.BlockSpec(memory_space=pltpu.SMEM),
        out_shape=jax.ShapeDtypeStruct((n_rows,), I32),
        compiler_params=_cp(("arbitrary",)),
        name="moe_sources",
    )(pos_flat)


W_CHUNKS = 16


def _moe_up_body(te_ref, nu_ref, clo_ref, chi_ref, nxt_ref, slot_ref, src_ref, hn_ref, w_ref, b_ref, o_ref,
                 xf_ref, xb_ref, wb_ref, st_ref, gsem, wsem, *, tm, de):
    i = pl.program_id(0)
    ch = w_ref.shape[1] // W_CHUNKS

    def w_copy(e, c, buf):
        return pltpu.make_async_copy(w_ref.at[e, pl.ds(c * ch, ch)], st_ref.at[buf], wsem.at[buf])

    def convert(c, buf, slot):
        wb_ref[slot, pl.ds(pl.multiple_of(c * ch, ch), ch), :] = st_ref[buf].astype(BF16)

    def gather(tile):
        base = tile * tm

        def rows8(g, carry):
            r0 = pl.multiple_of(g * SUBLANE, SUBLANE)
            for j in range(SUBLANE):
                pltpu.make_async_copy(hn_ref.at[pl.ds(src_ref[base + r0 + j], 1)],
                                      xf_ref.at[pl.ds(r0 + j, 1)], gsem).start()
            return carry

        lax.fori_loop(0, tm // SUBLANE, rows8, 0)

    @pl.when(i == 0)
    def _():
        e0, s0 = te_ref[0], slot_ref[0]
        w_copy(e0, 0, 0).start()

        def load(c, carry):
            @pl.when(c + 1 < W_CHUNKS)
            def _():
                w_copy(e0, c + 1, (c + 1) % 2).start()

            w_copy(e0, c, c % 2).wait()
            convert(c, c % 2, s0)
            return carry

        lax.fori_loop(0, W_CHUNKS, load, 0)
        gather(0)

    @pl.when(i < nu_ref[0])
    def _():
        pltpu.make_async_copy(hn_ref.at[pl.ds(0, tm)], xf_ref, gsem).wait()
        xb_ref[...] = xf_ref[...].astype(BF16)

        @pl.when(i + 1 < nu_ref[0])
        def _():
            gather(i + 1)

        e_n, s_n = nxt_ref[i], 1 - slot_ref[i]
        lo, hi = clo_ref[i], chi_ref[i]

        @pl.when((lo == 0) & (hi > 0))
        def _():
            w_copy(e_n, 0, 0).start()
            w_copy(e_n, 1, 1).start()

        def chunk(c, carry):
            w_copy(e_n, c, c % 2).wait()
            convert(c, c % 2, s_n)

            @pl.when(c + 2 < W_CHUNKS)
            def _():
                w_copy(e_n, c + 2, c % 2).start()

            return carry

        lax.fori_loop(lo, hi, chunk, 0)

        gu = _dot(xb_ref[...], wb_ref[slot_ref[i]]) + b_ref[...]
        gate = jnp.minimum(gu[:, :de], SWIGLU_LIMIT)
        up = jnp.clip(gu[:, de:], -SWIGLU_LIMIT, SWIGLU_LIMIT)
        o_ref[...] = ((up + 1.0) * (gate * jax.nn.sigmoid(SWIGLU_ALPHA * gate))).astype(o_ref.dtype)

    @pl.when(i >= nu_ref[0])
    def _():
        o_ref[...] = jnp.zeros_like(o_ref)


def _moe_down_body(te_ref, nu_ref, a_ref, w_ref, b_ref, o_ref, wb_ref):
    i = pl.program_id(0)

    @pl.when(i < nu_ref[0])
    def _():
        @pl.when((i == 0) | (te_ref[i] != te_ref[jnp.maximum(i - 1, 0)]))
        def _():
            wb_ref[...] = w_ref[...].astype(BF16)

        o_ref[...] = _dot(a_ref[...], wb_ref[...]) + b_ref[...]

    @pl.when(i >= nu_ref[0])
    def _():
        o_ref[...] = jnp.zeros_like(o_ref)


def _weight_stream_schedule(tile_expert, n_used, tiles_per_expert, group_start_tile):
    ne = tiles_per_expert.shape[0]
    n_tiles = tile_expert.shape[0]
    nonempty = tiles_per_expert > 0
    ordinal = jnp.cumsum(nonempty.astype(I32)) - 1
    cand = jnp.where(nonempty, jnp.arange(ne, dtype=I32), ne)
    next_ge = lax.cummin(cand, axis=0, reverse=True)
    next_gt = jnp.concatenate([next_ge[1:], jnp.full((1,), ne, I32)])
    tile = jnp.arange(n_tiles, dtype=I32)
    k_in = tile - group_start_tile[tile_expert]
    n_in = jnp.maximum(tiles_per_expert[tile_expert], 1)
    nxt = next_gt[tile_expert]
    live = (nxt < ne) & (tile < n_used[0])
    c_lo = jnp.where(live, k_in * W_CHUNKS // n_in, 0).astype(I32)
    c_hi = jnp.where(live, (k_in + 1) * W_CHUNKS // n_in, 0).astype(I32)
    return c_lo, c_hi, jnp.where(live, nxt, 0).astype(I32), (ordinal[tile_expert] % 2).astype(I32)


def moe_experts(hn, src, tile_expert, n_used, sched, w_gu, b_gu, w_dn, b_dn, tm):
    d = hn.shape[1]
    n_rows = src.shape[0]
    ne, _, de2 = w_gu.shape
    de = de2 // 2
    n_tiles = n_rows // tm
    assert d % W_CHUNKS == 0 and W_CHUNKS >= 2
    c_lo, c_hi, nxt, slot = sched
    act = pl.pallas_call(
        functools.partial(_moe_up_body, tm=tm, de=de),
        grid_spec=pltpu.PrefetchScalarGridSpec(
            num_scalar_prefetch=7,
            grid=(n_tiles,),
            in_specs=[pl.BlockSpec(memory_space=pl.ANY),
                      pl.BlockSpec(memory_space=pl.ANY),
                      pl.BlockSpec((None, 1, de2), lambda i, te, *_: (te[i], 0, 0))],
            out_specs=pl.BlockSpec((tm, de), lambda i, *_: (i, 0)),
            scratch_shapes=[pltpu.VMEM((tm, d), F32), pltpu.VMEM((tm, d), BF16),
                            pltpu.VMEM((2, d, de2), BF16), pltpu.VMEM((2, d // W_CHUNKS, de2), F32),
                            pltpu.SemaphoreType.DMA(()), pltpu.SemaphoreType.DMA((2,))],
        ),
        out_shape=jax.ShapeDtypeStruct((n_rows, de), BF16),
        compiler_params=_cp(("arbitrary",)),
        name="moe_up",
    )(tile_expert, n_used, c_lo, c_hi, nxt, slot, src, hn, w_gu, b_gu.reshape(ne, 1, de2))
    return pl.pallas_call(
        _moe_down_body,
        grid_spec=pltpu.PrefetchScalarGridSpec(
            num_scalar_prefetch=2,
            grid=(n_tiles,),
            in_specs=[pl.BlockSpec((tm, de), lambda i, te, nu: (i, 0)),
                      pl.BlockSpec((None, de, d), lambda i, te, nu: (te[i], 0, 0)),
                      pl.BlockSpec((None, 1, d), lambda i, te, nu: (te[i], 0, 0))],
            out_specs=pl.BlockSpec((tm, d), lambda i, te, nu: (i, 0)),
            scratch_shapes=[pltpu.VMEM((de, d), BF16)],
        ),
        out_shape=jax.ShapeDtypeStruct((n_rows, d), F32),
        compiler_params=_cp(("arbitrary",)),
        name="moe_down",
    )(tile_expert, n_used, act, w_dn, b_dn.reshape(ne, 1, d))


def _combine_body(pos_ref, h_ref, cw_ref, g_ref, ys_ref, h2_ref, hn_ref, buf_ref, sem, *, tm):
    i = pl.program_id(0)

    def issue(tile, slot):
        base = tile * (tm * TOP_K)

        def rows8(g, carry):
            r0 = pl.multiple_of(g * SUBLANE, SUBLANE)
            for j in range(SUBLANE):
                for k in range(TOP_K):
                    p = pos_ref[base + (r0 + j) * TOP_K + k]
                    pltpu.make_async_copy(ys_ref.at[pl.ds(p, 1)], buf_ref.at[slot, k, pl.ds(r0 + j, 1)],
                                          sem.at[slot]).start()
            return carry

        lax.fori_loop(0, tm // SUBLANE, rows8, 0)

    @pl.when(i == 0)
    def _():
        issue(0, 0)

    @pl.when(i + 1 < pl.num_programs(0))
    def _():
        issue(i + 1, (i + 1) % 2)

    slot = i % 2
    for k in range(TOP_K):
        pltpu.make_async_copy(ys_ref.at[pl.ds(0, tm)], buf_ref.at[slot, k], sem.at[slot]).wait()
    acc = h_ref[...]
    cw = cw_ref[...]
    for k in range(TOP_K):
        acc = acc + cw[:, k:k + 1] * buf_ref[slot, k]
    h2_ref[...] = acc
    hn_ref[...] = _rms(acc, g_ref[...]).astype(hn_ref.dtype)


def moe_combine(h, cw, ys, pos_flat, norm_g, tm):
    t, d = h.shape
    grid_spec = pltpu.PrefetchScalarGridSpec(
        num_scalar_prefetch=1,
        grid=(t // tm,),
        in_specs=[pl.BlockSpec((tm, d), lambda i, pos: (i, 0)),
                  pl.BlockSpec((tm, LANE), lambda i, pos: (i, 0)),
                  pl.BlockSpec((1, d), lambda i, pos: (0, 0)),
                  pl.BlockSpec(memory_space=pl.ANY)],
        out_specs=[pl.BlockSpec((tm, d), lambda i, pos: (i, 0)),
                   pl.BlockSpec((tm, d), lambda i, pos: (i, 0))],
        scratch_shapes=[pltpu.VMEM((2, TOP_K, tm, d), F32), pltpu.SemaphoreType.DMA((2,))],
    )
    return pl.pallas_call(
        functools.partial(_combine_body, tm=tm),
        grid_spec=grid_spec,
        out_shape=[jax.ShapeDtypeStruct((t, d), F32), jax.ShapeDtypeStruct((t, d), BF16)],
        compiler_params=_cp(("arbitrary",)),
        name="moe_combine",
    )(pos_flat, h, cw, norm_g.reshape(1, -1).astype(F32), ys)


def _ple_body(a_ref, w_ref, p_ref, wp_ref, h_ref, o_ref):
    gate = _dot(a_ref[...], w_ref[...])
    pp = _dot(p_ref[...].astype(BF16), wp_ref[...])
    o_ref[...] = h_ref[...] + jax.nn.sigmoid(gate) * pp


def ple_gate(hn, w_gate, p, w_proj, h, *, tm=1024, tn=512):
    t, d = h.shape
    pd = p.shape[1]
    tm, tn = _tile(t, tm), _tile(d, tn)
    return pl.pallas_call(
        _ple_body,
        grid=(t // tm, d // tn),
        in_specs=[pl.BlockSpec((tm, d), lambda i, j: (i, 0)),
                  pl.BlockSpec((d, tn), lambda i, j: (0, j)),
                  pl.BlockSpec((tm, pd), lambda i, j: (i, 0)),
                  pl.BlockSpec((pd, tn), lambda i, j: (0, j)),
                  pl.BlockSpec((tm, tn), lambda i, j: (i, j))],
        out_specs=pl.BlockSpec((tm, tn), lambda i, j: (i, j)),
        out_shape=jax.ShapeDtypeStruct((t, d), F32),
        compiler_params=_cp(("parallel", "arbitrary")),
        name="ple_gate",
    )(hn, w_gate, p, w_proj, h)


def _proj_layout(d_model, qkw, vw, qr, kvr):
    lay, off = {}, 0
    for name, width in (("gq", qkw), ("gk", qkw), ("gv", vw), ("gg", vw), ("cq", qr), ("ckv", kvr),
                        ("krA", LANE), ("krB", LANE), ("gf", LANE)):
        lay[name] = off
        off += width
    gate_tile = _tile(d_model, 1024)
    off = -(-off // gate_tile) * gate_tile
    lay["gate_gla"] = off
    lay["gate_mla"] = off + d_model
    lay["total"] = off + 2 * d_model
    lay["gate_tile"] = gate_tile
    return lay


def _reorder_w_in(w_in, lay, qkw, vw, rank, qr, kvr, d_model):
    splits, off = [], 0
    for width in (qkw, qkw, vw, vw, rank, qr, kvr, MLA_ROPE, d_model, d_model):
        splits.append(w_in[:, off:off + width])
        off += width
    gq, gk, gv, gg, gf, cq, ckv, kr, g_gla, g_mla = splits
    front = jnp.concatenate([gq, gk, gv, gg, cq, ckv, _pad_lane(kr), _pad_lane(_swap_halves(kr)),
                             _pad_lane(gf)], axis=1)
    front = jnp.pad(front, ((0, 0), (0, lay["gate_gla"] - front.shape[1])))
    return jnp.concatenate([front, g_gla, g_mla], axis=1).astype(BF16)


def _layer(h, p, w_in, gla_gate_up, gla_gate_bias, gla_out_norm, mla_q_norm, mla_kv_norm, mla_w_uq,
           mla_w_ukv, w_branch_gla, w_branch_mla, w_out, norm_mix, norm_moe, router_w, router_b,
           expert_w_gate_up, expert_b_gate_up, expert_w_down, expert_b_down, norm_ple, ple_gate_w,
           ple_proj_w, batch, seq):
    t, d = h.shape
    rank, qkw = gla_gate_up.shape
    vw = w_branch_gla.shape[0]
    qr, kvr = mla_q_norm.shape[0], mla_kv_norm.shape[0]
    lay = _proj_layout(d, qkw, vw, qr, kvr)

    xn = rms_norm(h, norm_mix)
    proj = matmul(xn, _reorder_w_in(w_in, lay, qkw, vw, rank, qr, kvr, d), out_dtype=BF16,
                  tn=lay["gate_tile"], name="in_proj")
    o_gla = gla_mixer(proj, lay, gla_gate_up, gla_gate_bias, gla_out_norm, batch, seq)
    o_mla = mla_mixer(proj, lay, mla_q_norm, mla_kv_norm, mla_w_uq, mla_w_ukv, batch, seq)
    merged = merge_branches(o_gla, o_mla, w_branch_gla.astype(BF16), w_branch_mla.astype(BF16), proj, lay, d)
    h1 = matmul_residual(merged, w_out.astype(BF16), h, name="out_proj")

    ne = router_w.shape[1]
    tm_e = _tile(t, 256)
    hn, idx, rank_in, cw, cnt = moe_router(h1, norm_moe, router_w, router_b, tm_e)
    counts = cnt[0, :ne].astype(I32)
    tiles_e = -(-counts // tm_e)
    gend = jnp.cumsum(tiles_e) * tm_e
    gstart = gend - tiles_e * tm_e
    pos = (gstart[idx[:, :TOP_K]] + rank_in[:, :TOP_K]).reshape(-1)
    n_tiles = t * TOP_K // tm_e + ne
    tile_start = jnp.arange(n_tiles, dtype=I32) * tm_e
    tile_expert = jnp.minimum(jnp.sum((gend[None, :] <= tile_start[:, None]).astype(I32), axis=1), ne - 1)
    n_used = (gend[-1:] // tm_e).astype(I32)
    sched = _weight_stream_schedule(tile_expert, n_used, tiles_e, gstart // tm_e)
    src = moe_sources(pos, n_tiles * tm_e)
    ys = moe_experts(hn, src, tile_expert, n_used, sched, expert_w_gate_up, expert_b_gate_up,
                     expert_w_down, expert_b_down, tm_e)
    h2, hn2 = moe_combine(h1, cw, ys, pos, norm_ple, _tile(t, 128))

    return ple_gate(hn2, ple_gate_w.astype(BF16), p, ple_proj_w.astype(BF16), h2)


def kernel(x, p, w_in, gla_gate_up, gla_gate_bias, gla_out_norm, mla_q_norm, mla_kv_norm, mla_w_uq, mla_w_ukv, w_branch_gla, w_branch_mla, w_out, norm_mix, norm_moe, router_w, router_b, expert_w_gate_up, expert_b_gate_up, expert_w_down, expert_b_down, norm_ple, ple_gate_w, ple_proj_w, final_norm):
    batch, seq, d = x.shape
    h = x.reshape(batch * seq, d)
    for i in range(p.shape[0]):
        h = _layer(h, p[i].reshape(batch * seq, -1), w_in[i], gla_gate_up[i], gla_gate_bias[i],
                   gla_out_norm[i], mla_q_norm[i], mla_kv_norm[i], mla_w_uq[i], mla_w_ukv[i],
                   w_branch_gla[i], w_branch_mla[i], w_out[i], norm_mix[i], norm_moe[i], router_w[i],
                   router_b[i], expert_w_gate_up[i], expert_b_gate_up[i], expert_w_down[i],
                   expert_b_down[i], norm_ple[i], ple_gate_w[i], ple_proj_w[i], batch, seq)
    return rms_norm(h, final_norm, out_dtype=F32).reshape(batch, seq, d)
```

```python
import functools
import math

import jax
import jax.numpy as jnp
from jax import lax
from jax.experimental import pallas as pl
from jax.experimental.pallas import tpu as pltpu

F32 = jnp.float32
BF16 = jnp.bfloat16
U32 = jnp.uint32
I32 = jnp.int32

CHUNK = 64
EPS = 1e-6
GLA_HEADS = 8
GLA_TAU = 16.0
MLA_HEADS = 16
MLA_NOPE = 128
MLA_ROPE = 64
MLA_V = 128
ROPE_THETA = 10000.0
TOP_K = 4
SWIGLU_LIMIT = 7.0
SWIGLU_ALPHA = 1.702

LANE = 128
SUBLANE = 8
VMEM_LIMIT = 56 * 2 ** 20
NEG_BIG = -1e30


def _tile(n, pref):
    if n <= pref:
        return n
    t = pref
    while t >= 8:
        if n % t == 0 and t % 8 == 0:
            return t
        t -= 8
    return n


def _cp(sem):
    return pltpu.CompilerParams(dimension_semantics=sem, vmem_limit_bytes=VMEM_LIMIT)


def _dot(a, b):
    return jnp.dot(a, b, preferred_element_type=F32)


def _dot_nt(a, b):
    return lax.dot_general(a, b, (((1,), (1,)), ((), ())), preferred_element_type=F32)


def _dot_tn(a, b):
    return lax.dot_general(a, b, (((0,), (0,)), ((), ())), preferred_element_type=F32)


def _rms(x, g):
    return x * lax.rsqrt(jnp.mean(x * x, axis=-1, keepdims=True) + EPS) * g


def _rms_body(x_ref, g_ref, o_ref):
    o_ref[...] = _rms(x_ref[...].astype(F32), g_ref[...]).astype(o_ref.dtype)


def rms_norm(x, g, *, col_block=0, width=None, out_dtype=BF16, tm=512):
    m = x.shape[0]
    width = width or x.shape[1]
    tm = _tile(m, tm)
    return pl.pallas_call(
        _rms_body,
        grid=(m // tm,),
        in_specs=[pl.BlockSpec((tm, width), lambda i: (i, col_block)),
                  pl.BlockSpec((1, width), lambda i: (0, 0))],
        out_specs=pl.BlockSpec((tm, width), lambda i: (i, 0)),
        out_shape=jax.ShapeDtypeStruct((m, width), out_dtype),
        compiler_params=_cp(("parallel",)),
        name="rms_norm",
    )(x, g.reshape(1, -1).astype(F32))


def _mm_body(a_ref, w_ref, o_ref):
    o_ref[...] = _dot(a_ref[...], w_ref[...]).astype(o_ref.dtype)


def matmul(a, w, *, out_dtype, tm=1024, tn=1024, name="matmul"):
    m, k = a.shape
    n = w.shape[1]
    tm, tn = _tile(m, tm), _tile(n, tn)
    return pl.pallas_call(
        _mm_body,
        grid=(m // tm, n // tn),
        in_specs=[pl.BlockSpec((tm, k), lambda i, j: (i, 0)),
                  pl.BlockSpec((k, tn), lambda i, j: (0, j))],
        out_specs=pl.BlockSpec((tm, tn), lambda i, j: (i, j)),
        out_shape=jax.ShapeDtypeStruct((m, n), out_dtype),
        compiler_params=_cp(("parallel", "arbitrary")),
        name=name,
    )(a, w)


def _gla_body(q_ref, k_ref, v_ref, gg_ref, gf_ref, gup_ref, gb_ref, on_ref, tri_ref, blk_ref,
              o_ref, st_ref, ob_ref, *, nchunk, scale):
    @pl.when(pl.program_id(2) == 0)
    def _():
        st_ref[...] = jnp.zeros_like(st_ref)

    logit = _dot(gf_ref[...], gup_ref[...]) + gb_ref[...]
    la = (jnp.minimum(logit, 0.0) - jnp.log(1.0 + jnp.exp(-jnp.abs(logit)))) * (1.0 / GLA_TAU)
    la_hi = la.astype(BF16)
    la_lo = (la - la_hi.astype(F32)).astype(BF16)
    tri = tri_ref[...]
    blk = blk_ref[...]
    b = _dot(tri, la_hi) + _dot(tri, la_lo)
    bl = _dot(blk, la_hi) + _dot(blk, la_lo)
    q = q_ref[...].astype(F32)
    k = k_ref[...].astype(F32)
    qd = (q * scale * jnp.exp(b)).astype(BF16)
    kd = (k * jnp.exp(-b)).astype(BF16)
    ke = (k * jnp.exp(bl - b)).astype(BF16)
    dec = jnp.exp(bl)
    row = lax.broadcasted_iota(I32, (CHUNK, CHUNK), 0)
    col = lax.broadcasted_iota(I32, (CHUNK, CHUNK), 1)
    causal = row >= col
    for c in range(nchunk):
        lo = c * CHUNK
        qc = qd[lo:lo + CHUNK]
        kc = kd[lo:lo + CHUNK]
        kec = ke[lo:lo + CHUNK]
        vc = v_ref[lo:lo + CHUNK, :]
        s = jnp.where(causal, _dot_nt(qc, kc), 0.0).astype(BF16)
        st = st_ref[...]
        ob_ref[lo:lo + CHUNK, :] = _dot(s, vc) + _dot_nt(qc, st.astype(BF16))
        st_ref[...] = st * dec[lo:lo + 1, :] + _dot_tn(vc, kec)
    o = ob_ref[...]
    o = _rms(o, on_ref[...])
    g = gg_ref[...].astype(F32)
    o_ref[...] = (o * (g * jax.nn.sigmoid(g))).astype(o_ref.dtype)


def gla_mixer(proj, proj_mid, lay, gate_up, gate_bias, out_norm, batch, seq):
    heads = GLA_HEADS
    dk = gate_up.shape[1] // heads
    dv = out_norm.shape[0]
    rank = gate_up.shape[0]
    lb = _tile(seq, 512)
    nchunk = lb // CHUNK
    nsb = seq // lb
    t = batch * seq
    gup = jnp.zeros((LANE, heads * dk), F32).at[:rank].set(gate_up).astype(BF16)
    r = jnp.arange(lb)
    same = (r[:, None] // CHUNK) == (r[None, :] // CHUNK)
    tri = (same & (r[None, :] <= r[:, None])).astype(BF16)
    blk = same.astype(BF16)
    qb, kb = lay["gq"] // dk, lay["gk"] // dk
    vb, gb = lay["gv"] // dv, lay["gg"] // dv
    fb = lay["gf"] // LANE
    rowmap = lambda b, h, s: b * nsb + s
    return pl.pallas_call(
        functools.partial(_gla_body, nchunk=nchunk, scale=dk ** -0.5),
        grid=(batch, heads, nsb),
        in_specs=[
            pl.BlockSpec((lb, dk), lambda b, h, s: (rowmap(b, h, s), qb + h)),
            pl.BlockSpec((lb, dk), lambda b, h, s: (rowmap(b, h, s), kb + h)),
            pl.BlockSpec((lb, dv), lambda b, h, s: (rowmap(b, h, s), vb + h)),
            pl.BlockSpec((lb, dv), lambda b, h, s: (rowmap(b, h, s), gb + h)),
            pl.BlockSpec((lb, LANE), lambda b, h, s: (rowmap(b, h, s), fb)),
            pl.BlockSpec((LANE, dk), lambda b, h, s: (0, h)),
            pl.BlockSpec((1, dk), lambda b, h, s: (0, h)),
            pl.BlockSpec((1, dv), lambda b, h, s: (0, 0)),
            pl.BlockSpec((lb, lb), lambda b, h, s: (0, 0)),
            pl.BlockSpec((lb, lb), lambda b, h, s: (0, 0)),
        ],
        out_specs=pl.BlockSpec((lb, dv), lambda b, h, s: (rowmap(b, h, s), h)),
        out_shape=jax.ShapeDtypeStruct((t, heads * dv), BF16),
        scratch_shapes=[pltpu.VMEM((dv, dk), F32), pltpu.VMEM((lb, dv), F32)],
        compiler_params=_cp(("parallel", "parallel", "arbitrary")),
        name="gla_mixer",
    )(proj, proj, proj, proj, proj_mid, gup, gate_bias.reshape(1, -1).astype(F32),
      out_norm.reshape(1, -1).astype(F32), tri, blk)


def _q_up_body(a_ref, w_ref, c_ref, s_ref, o_ref, *, hpt, scale):
    acc = _dot(a_ref[...], w_ref[...])
    c = c_ref[...]
    s = s_ref[...]
    for h in range(hpt):
        src = h * 3 * LANE
        dst = h * 2 * LANE
        o_ref[:, dst:dst + LANE] = (acc[:, src:src + LANE] * scale).astype(o_ref.dtype)
        pe = acc[:, src + LANE:src + 2 * LANE] * c + acc[:, src + 2 * LANE:src + 3 * LANE] * s
        o_ref[:, dst + LANE:dst + 2 * LANE] = (pe * scale).astype(o_ref.dtype)


def _k_up_body(a_ref, w_ref, kra_ref, krb_ref, c_ref, s_ref, o_ref, *, hpt):
    acc = _dot(a_ref[...], w_ref[...])
    kpe = (kra_ref[...].astype(F32) * c_ref[...] + krb_ref[...].astype(F32) * s_ref[...]).astype(o_ref.dtype)
    for h in range(hpt):
        o_ref[:, 2 * h * LANE:(2 * h + 1) * LANE] = acc[:, h * LANE:(h + 1) * LANE].astype(o_ref.dtype)
        o_ref[:, (2 * h + 1) * LANE:(2 * h + 2) * LANE] = kpe


def _v_up_body(a_ref, w_ref, o_ref, *, hpt):
    acc = _dot(a_ref[...], w_ref[...])
    lane = lax.broadcasted_iota(I32, (acc.shape[0], LANE), 1)
    ones_col = (lane == 0).astype(o_ref.dtype)
    for h in range(hpt):
        o_ref[:, 2 * h * LANE:(2 * h + 1) * LANE] = acc[:, h * LANE:(h + 1) * LANE].astype(o_ref.dtype)
        o_ref[:, (2 * h + 1) * LANE:(2 * h + 2) * LANE] = ones_col


def _swap_halves(w):
    half = w.shape[-1] // 2
    return jnp.concatenate([w[..., half:], w[..., :half]], axis=-1)


def _pad_lane(w):
    pad = LANE - w.shape[-1]
    return jnp.pad(w, [(0, 0)] * (w.ndim - 1) + [(0, pad)])


def mla_mixer(proj, lay, q_norm, kv_norm, w_uq, w_ukv, batch, seq):
    heads = MLA_HEADS
    t = batch * seq
    qr, kvr = q_norm.shape[0], kv_norm.shape[0]
    assert MLA_NOPE == LANE and MLA_V == LANE and MLA_ROPE <= LANE
    scale = (MLA_NOPE + MLA_ROPE) ** -0.5 * math.log2(math.e)
    tm = _tile(seq, 1024)
    nrb = seq // tm
    hpt = 4 if heads % 4 == 0 else 1

    pos = jnp.arange(seq, dtype=F32)
    inv_freq = ROPE_THETA ** (-jnp.arange(0, MLA_ROPE, 2, dtype=F32) / MLA_ROPE)
    ang = pos[:, None] * inv_freq[None, :]
    cos, sin = jnp.cos(ang), jnp.sin(ang)
    ctab = _pad_lane(jnp.concatenate([cos, cos], axis=-1))
    stab = _pad_lane(jnp.concatenate([-sin, sin], axis=-1))

    cqn = rms_norm(proj, q_norm, col_block=lay["cq"] // qr, width=qr)
    ckvn = rms_norm(proj, kv_norm, col_block=lay["ckv"] // kvr, width=kvr)

    wq = w_uq.reshape(qr, heads, MLA_NOPE + MLA_ROPE)
    wq_pe = wq[..., MLA_NOPE:]
    wq_r = jnp.concatenate([wq[..., :MLA_NOPE], _pad_lane(wq_pe), _pad_lane(_swap_halves(wq_pe))],
                           axis=-1).reshape(qr, heads * 3 * LANE).astype(BF16)
    q_cat = pl.pallas_call(
        functools.partial(_q_up_body, hpt=hpt, scale=scale),
        grid=(t // tm, heads // hpt),
        in_specs=[pl.BlockSpec((tm, qr), lambda i, j: (i, 0)),
                  pl.BlockSpec((qr, hpt * 3 * LANE), lambda i, j: (0, j)),
                  pl.BlockSpec((tm, LANE), lambda i, j: (i % nrb, 0)),
                  pl.BlockSpec((tm, LANE), lambda i, j: (i % nrb, 0))],
        out_specs=pl.BlockSpec((tm, hpt * 2 * LANE), lambda i, j: (i, j)),
        out_shape=jax.ShapeDtypeStruct((t, heads * 2 * LANE), BF16),
        compiler_params=_cp(("parallel", "arbitrary")),
        name="mla_q_up",
    )(cqn, wq_r, ctab, stab)

    wkv = w_ukv.reshape(kvr, heads, MLA_NOPE + MLA_V)
    wk = wkv[..., :MLA_NOPE].reshape(kvr, heads * MLA_NOPE).astype(BF16)
    wv = wkv[..., MLA_NOPE:].reshape(kvr, heads * MLA_V).astype(BF16)
    ka, kb = lay["krA"] // LANE, lay["krB"] // LANE
    k_cat = pl.pallas_call(
        functools.partial(_k_up_body, hpt=hpt),
        grid=(t // tm, heads // hpt),
        in_specs=[pl.BlockSpec((tm, kvr), lambda i, j: (i, 0)),
                  pl.BlockSpec((kvr, hpt * LANE), lambda i, j: (0, j)),
                  pl.BlockSpec((tm, LANE), lambda i, j: (i, ka)),
                  pl.BlockSpec((tm, LANE), lambda i, j: (i, kb)),
                  pl.BlockSpec((tm, LANE), lambda i, j: (i % nrb, 0)),
                  pl.BlockSpec((tm, LANE), lambda i, j: (i % nrb, 0))],
        out_specs=pl.BlockSpec((tm, hpt * 2 * LANE), lambda i, j: (i, j)),
        out_shape=jax.ShapeDtypeStruct((t, heads * 2 * LANE), BF16),
        compiler_params=_cp(("parallel", "arbitrary")),
        name="mla_k_up",
    )(ckvn, wk, proj, proj, ctab, stab)
    v_ext = pl.pallas_call(
        functools.partial(_v_up_body, hpt=hpt),
        grid=(t // tm, heads // hpt),
        in_specs=[pl.BlockSpec((tm, kvr), lambda i, j: (i, 0)),
                  pl.BlockSpec((kvr, hpt * LANE), lambda i, j: (0, j))],
        out_specs=pl.BlockSpec((tm, hpt * 2 * LANE), lambda i, j: (i, j)),
        out_shape=jax.ShapeDtypeStruct((t, heads * 2 * LANE), BF16),
        compiler_params=_cp(("parallel", "arbitrary")),
        name="mla_v_up",
    )(ckvn, wv)
    return mla_attention(q_cat, k_cat, v_ext, batch, seq)


def _mla_attn_body(q_ref, k_ref, v_ref, o_ref, m_ref, acc_ref, *, tq, hg):
    qi = pl.program_id(2)
    m_ref[...] = jnp.full_like(m_ref, NEG_BIG)
    acc_ref[...] = jnp.zeros_like(acc_ref)
    w = 2 * LANE

    def block(ki, masked):
        ks = pl.multiple_of(ki * tq, tq)
        for g in range(hg):
            s = _dot_nt(q_ref[:, g * w:(g + 1) * w], k_ref[pl.ds(ks, tq), g * w:(g + 1) * w])
            if masked:
                qc = lax.broadcasted_iota(I32, (tq, tq), 0) // CHUNK
                kc = lax.broadcasted_iota(I32, (tq, tq), 1) // CHUNK
                s = jnp.where(kc <= qc, s, NEG_BIG)
            m_prev = m_ref[g]
            m_new = jnp.maximum(m_prev, jnp.max(s, axis=-1, keepdims=True))
            p = jnp.exp2(s - m_new).astype(BF16)
            acc_ref[g] = (jnp.exp2(m_prev - m_new) * acc_ref[g]
                          + _dot(p, v_ref[pl.ds(ks, tq), g * w:(g + 1) * w]))
            m_ref[g] = m_new

    def body(ki, carry):
        block(ki, False)
        return carry

    lax.fori_loop(0, qi, body, 0)
    block(qi, True)
    for g in range(hg):
        acc = acc_ref[g]
        o_ref[:, g * LANE:(g + 1) * LANE] = (acc[:, :LANE] / acc[:, LANE:LANE + 1]).astype(o_ref.dtype)


def mla_attention(q_cat, k_cat, v_ext, batch, seq):
    heads = MLA_HEADS
    t = batch * seq
    tq = _tile(seq, 512)
    nq = seq // tq
    hg = 4 if heads % 4 == 0 else 1
    w = 2 * LANE
    return pl.pallas_call(
        functools.partial(_mla_attn_body, tq=tq, hg=hg),
        grid=(batch, heads // hg, nq),
        in_specs=[pl.BlockSpec((tq, hg * w), lambda b, h, i: (b * nq + i, h)),
                  pl.BlockSpec((seq, hg * w), lambda b, h, i: (b, h)),
                  pl.BlockSpec((seq, hg * w), lambda b, h, i: (b, h))],
        out_specs=pl.BlockSpec((tq, hg * LANE), lambda b, h, i: (b * nq + i, h)),
        out_shape=jax.ShapeDtypeStruct((t, heads * MLA_V), BF16),
        scratch_shapes=[pltpu.VMEM((hg, tq, 1), F32), pltpu.VMEM((hg, tq, w), F32)],
        compiler_params=_cp(("parallel", "parallel", "arbitrary")),
        name="mla_attention",
    )(q_cat, k_cat, v_ext)


def _merge_body(a1_ref, w1_ref, a2_ref, w2_ref, g1_ref, g2_ref, o_ref):
    y1 = _dot(a1_ref[...], w1_ref[...])
    y2 = _dot(a2_ref[...], w2_ref[...])
    o_ref[...] = (jax.nn.sigmoid(g1_ref[...].astype(F32)) * y1
                  + jax.nn.sigmoid(g2_ref[...].astype(F32)) * y2).astype(o_ref.dtype)


def merge_branches(o_gla, o_mla, w_gla, w_mla, proj, lay, d_model):
    t = o_gla.shape[0]
    tm = _tile(t, 1024)
    tn = _tile(d_model, 512)
    g1, g2 = lay["gate_gla"] // tn, lay["gate_mla"] // tn
    k1, k2 = o_gla.shape[1], o_mla.shape[1]
    return pl.pallas_call(
        _merge_body,
        grid=(t // tm, d_model // tn),
        in_specs=[pl.BlockSpec((tm, k1), lambda i, j: (i, 0)),
                  pl.BlockSpec((k1, tn), lambda i, j: (0, j)),
                  pl.BlockSpec((tm, k2), lambda i, j: (i, 0)),
                  pl.BlockSpec((k2, tn), lambda i, j: (0, j)),
                  pl.BlockSpec((tm, tn), lambda i, j: (i, g1 + j)),
                  pl.BlockSpec((tm, tn), lambda i, j: (i, g2 + j))],
        out_specs=pl.BlockSpec((tm, tn), lambda i, j: (i, j)),
        out_shape=jax.ShapeDtypeStruct((t, d_model), BF16),
        compiler_params=_cp(("parallel", "arbitrary")),
        name="merge_branches",
    )(o_gla, w_gla, o_mla, w_mla, proj, proj)


def _mm_res_body(a_ref, w_ref, r_ref, o_ref):
    o_ref[...] = r_ref[...] + _dot(a_ref[...], w_ref[...])


def matmul_residual(a, w, res, *, tm=1024, tn=512, name="matmul_residual"):
    m, k = a.shape
    n = w.shape[1]
    tm, tn = _tile(m, tm), _tile(n, tn)
    return pl.pallas_call(
        _mm_res_body,
        grid=(m // tm, n // tn),
        in_specs=[pl.BlockSpec((tm, k), lambda i, j: (i, 0)),
                  pl.BlockSpec((k, tn), lambda i, j: (0, j)),
                  pl.BlockSpec((tm, tn), lambda i, j: (i, j))],
        out_specs=pl.BlockSpec((tm, tn), lambda i, j: (i, j)),
        out_shape=jax.ShapeDtypeStruct((m, n), F32),
        compiler_params=_cp(("parallel", "arbitrary")),
        name=name,
    )(a, w, res)


def _router_body(h_ref, g_ref, rwh_ref, rwl_ref, rb_ref, tri_ref, hn_ref, idx_ref, rank_ref, cw_ref, cnt_ref,
                 carry_ref):
    @pl.when(pl.program_id(0) == 0)
    def _():
        carry_ref[...] = jnp.zeros_like(carry_ref)

    hn = _rms(h_ref[...], g_ref[...])
    hn_ref[...] = hn
    hn_hi = hn.astype(BF16)
    hn_lo = (hn - hn_hi.astype(F32)).astype(BF16)
    logits = (_dot(hn_hi, rwh_ref[...]) + _dot(hn_hi, rwl_ref[...]) + _dot(hn_lo, rwh_ref[...])) + rb_ref[...]
    tm = logits.shape[0]
    lane = lax.broadcasted_iota(I32, (tm, LANE), 1)
    work = logits
    vals, sels, idxs = [], [], []
    for _ in range(TOP_K):
        mx = jnp.max(work, axis=-1, keepdims=True)
        ix = jnp.min(jnp.where(work == mx, lane, LANE), axis=-1, keepdims=True)
        sel = lane == ix
        work = jnp.where(sel, -jnp.inf, work)
        vals.append(mx)
        sels.append(sel)
        idxs.append(ix)
    es = [jnp.exp(v - vals[0]) for v in vals]
    den = es[0]
    for e in es[1:]:
        den = den + e
    hot = jnp.zeros((tm, LANE), F32)
    for sel in sels:
        hot = hot + sel.astype(F32)
    carry = carry_ref[0:1, :]
    rank = _dot(tri_ref[...], hot.astype(BF16)) + carry
    idx_out = jnp.zeros((tm, LANE), I32)
    rank_out = jnp.zeros((tm, LANE), F32)
    cw_out = jnp.zeros((tm, LANE), F32)
    for k in range(TOP_K):
        rk = jnp.sum(jnp.where(sels[k], rank, 0.0), axis=-1, keepdims=True)
        idx_out = jnp.where(lane == k, idxs[k], idx_out)
        rank_out = jnp.where(lane == k, rk, rank_out)
        cw_out = jnp.where(lane == k, es[k] / den, cw_out)
    idx_ref[...] = idx_out
    rank_ref[...] = rank_out.astype(I32)
    cw_ref[...] = cw_out
    new_carry = carry + jnp.sum(hot, axis=0, keepdims=True)
    carry_ref[...] = jnp.broadcast_to(new_carry, carry_ref.shape)
    cnt_ref[...] = jnp.broadcast_to(new_carry, cnt_ref.shape)


def moe_router(h, norm_g, router_w, router_b, tm):
    t, d = h.shape
    ne = router_w.shape[1]
    assert ne <= LANE and d % (2 * LANE) == 0
    rw = jnp.zeros((d, LANE), F32).at[:, :ne].set(router_w)
    rw_hi = rw.astype(BF16)
    rw_lo = (rw - rw_hi.astype(F32)).astype(BF16)
    rb = jnp.full((1, LANE), NEG_BIG, F32).at[0, :ne].set(router_b)
    r = jnp.arange(tm)
    tri = (r[None, :] < r[:, None]).astype(BF16)
    row = lambda i: (i, 0)
    const = lambda i: (0, 0)
    return pl.pallas_call(
        _router_body,
        grid=(t // tm,),
        in_specs=[pl.BlockSpec((tm, d), row), pl.BlockSpec((1, d), const),
                  pl.BlockSpec((d, LANE), const), pl.BlockSpec((d, LANE), const),
                  pl.BlockSpec((1, LANE), const), pl.BlockSpec((tm, tm), const)],
        out_specs=[pl.BlockSpec((tm, d), row), pl.BlockSpec((tm, LANE), row),
                   pl.BlockSpec((tm, LANE), row), pl.BlockSpec((tm, LANE), row),
                   pl.BlockSpec((8, LANE), const)],
        out_shape=[jax.ShapeDtypeStruct((t, d), F32), jax.ShapeDtypeStruct((t, LANE), I32),
                   jax.ShapeDtypeStruct((t, LANE), I32), jax.ShapeDtypeStruct((t, LANE), F32),
                   jax.ShapeDtypeStruct((8, LANE), F32)],
        scratch_shapes=[pltpu.VMEM((8, LANE), F32)],
        compiler_params=_cp(("arbitrary",)),
        name="moe_router",
    )(h, norm_g.reshape(1, -1).astype(F32), rw_hi, rw_lo, rb, tri)


def _sources_body(pos_ref, src_ref, *, nsteps):
    i = pl.program_id(0)
    rows_per = src_ref.shape[0] // nsteps
    toks_per = pos_ref.shape[0] // TOP_K // nsteps

    @pl.when(i < nsteps)
    def _():
        def clear(r, carry):
            src_ref[i * rows_per + r] = 0
            return carry

        lax.fori_loop(0, rows_per, clear, 0, unroll=8)

    @pl.when(i >= nsteps)
    def _():
        t0 = (i - nsteps) * toks_per

        def put(j, carry):
            for k in range(TOP_K):
                src_ref[pos_ref[(t0 + j) * TOP_K + k]] = t0 + j
            return carry

        lax.fori_loop(0, toks_per, put, 0, unroll=2)


def moe_sources(pos_flat, n_rows):
    n_tok = pos_flat.shape[0] // TOP_K
    nsteps = 64 if (n_rows % 64 == 0 and n_tok % 64 == 0) else 1
    return pl.pallas_call(
        functools.partial(_sources_body, nsteps=nsteps),
        grid=(2 * nsteps,),
        in_specs=[pl.BlockSpec(memory_space=pltpu.SMEM)],
        out_specs=pl.BlockSpec(memory_space=pltpu.SMEM),
        out_shape=jax.ShapeDtypeStruct((n_rows,), I32),
        compiler_params=_cp(("arbitrary",)),
        name="moe_sources",
    )(pos_flat)


W_CHUNKS = 16


def _moe_up_body(te_ref, nu_ref, clo_ref, chi_ref, nxt_ref, slot_ref, src_ref, hn_ref, w_ref, b_ref, o_ref,
                 xf_ref, xb_ref, wb_ref, st_ref, gsem, wsem, *, tm, de):
    i = pl.program_id(0)
    ch = w_ref.shape[1] // W_CHUNKS

    def w_copy(e, c, buf):
        return pltpu.make_async_copy(w_ref.at[e, pl.ds(c * ch, ch)], st_ref.at[buf], wsem.at[buf])

    def convert(c, buf, slot):
        wb_ref[slot, pl.ds(pl.multiple_of(c * ch, ch), ch), :] = st_ref[buf].astype(BF16)

    def gather(tile):
        base = tile * tm

        def rows8(g, carry):
            r0 = pl.multiple_of(g * SUBLANE, SUBLANE)
            for j in range(SUBLANE):
                pltpu.make_async_copy(hn_ref.at[pl.ds(src_ref[base + r0 + j], 1)],
                                      xf_ref.at[pl.ds(r0 + j, 1)], gsem).start()
            return carry

        lax.fori_loop(0, tm // SUBLANE, rows8, 0)

    @pl.when(i == 0)
    def _():
        e0, s0 = te_ref[0], slot_ref[0]
        w_copy(e0, 0, 0).start()

        def load(c, carry):
            @pl.when(c + 1 < W_CHUNKS)
            def _():
                w_copy(e0, c + 1, (c + 1) % 2).start()

            w_copy(e0, c, c % 2).wait()
            convert(c, c % 2, s0)
            return carry

        lax.fori_loop(0, W_CHUNKS, load, 0)
        gather(0)

    @pl.when(i < nu_ref[0])
    def _():
        pltpu.make_async_copy(hn_ref.at[pl.ds(0, tm)], xf_ref, gsem).wait()
        xb_ref[...] = xf_ref[...].astype(BF16)

        e_n, s_n = nxt_ref[i], 1 - slot_ref[i]
        lo, hi = clo_ref[i], chi_ref[i]

        @pl.when((lo == 0) & (hi > 0))
        def _():
            w_copy(e_n, 0, 0).start()
            w_copy(e_n, 1, 1).start()

        def chunk(c, carry):
            w_copy(e_n, c, c % 2).wait()
            convert(c, c % 2, s_n)

            @pl.when(c + 2 < W_CHUNKS)
            def _():
                w_copy(e_n, c + 2, c % 2).start()

            return carry

        lax.fori_loop(lo, hi, chunk, 0)

        base = jnp.minimum(i + 1, nu_ref[0] - 1) * tm
        for r in range(tm):
            pltpu.make_async_copy(hn_ref.at[pl.ds(src_ref[base + r], 1)], xf_ref.at[pl.ds(r, 1)], gsem).start()
        gu = _dot(xb_ref[...], wb_ref[slot_ref[i]]) + b_ref[...]
        gate = jnp.minimum(gu[:, :de], SWIGLU_LIMIT)
        up = jnp.clip(gu[:, de:], -SWIGLU_LIMIT, SWIGLU_LIMIT)
        o_ref[...] = ((up + 1.0) * (gate * jax.nn.sigmoid(SWIGLU_ALPHA * gate))).astype(o_ref.dtype)

        @pl.when(i + 1 == nu_ref[0])
        def _():
            pltpu.make_async_copy(hn_ref.at[pl.ds(0, tm)], xf_ref, gsem).wait()

    @pl.when(i >= nu_ref[0])
    def _():
        o_ref[...] = jnp.zeros_like(o_ref)


def _moe_down_body(te_ref, nu_ref, a_ref, w_ref, b_ref, o_ref, wb_ref):
    i = pl.program_id(0)

    @pl.when(i < nu_ref[0])
    def _():
        @pl.when((i == 0) | (te_ref[i] != te_ref[jnp.maximum(i - 1, 0)]))
        def _():
            wb_ref[...] = w_ref[...].astype(BF16)

        o_ref[...] = _dot(a_ref[...], wb_ref[...]) + b_ref[...]

    @pl.when(i >= nu_ref[0])
    def _():
        o_ref[...] = jnp.zeros_like(o_ref)


def _weight_stream_schedule(tile_expert, n_used, tiles_per_expert, group_start_tile):
    ne = tiles_per_expert.shape[0]
    n_tiles = tile_expert.shape[0]
    nonempty = tiles_per_expert > 0
    ordinal = jnp.cumsum(nonempty.astype(I32)) - 1
    cand = jnp.where(nonempty, jnp.arange(ne, dtype=I32), ne)
    next_ge = lax.cummin(cand, axis=0, reverse=True)
    next_gt = jnp.concatenate([next_ge[1:], jnp.full((1,), ne, I32)])
    tile = jnp.arange(n_tiles, dtype=I32)
    k_in = tile - group_start_tile[tile_expert]
    n_in = jnp.maximum(tiles_per_expert[tile_expert], 1)
    nxt = next_gt[tile_expert]
    live = (nxt < ne) & (tile < n_used[0])
    c_lo = jnp.where(live, k_in * W_CHUNKS // n_in, 0).astype(I32)
    c_hi = jnp.where(live, (k_in + 1) * W_CHUNKS // n_in, 0).astype(I32)
    return c_lo, c_hi, jnp.where(live, nxt, 0).astype(I32), (ordinal[tile_expert] % 2).astype(I32)


def moe_experts(hn, src, tile_expert, n_used, sched, w_gu, b_gu, w_dn, b_dn, tm):
    d = hn.shape[1]
    n_rows = src.shape[0]
    ne, _, de2 = w_gu.shape
    de = de2 // 2
    n_tiles = n_rows // tm
    assert d % W_CHUNKS == 0 and W_CHUNKS >= 2
    c_lo, c_hi, nxt, slot = sched
    act = pl.pallas_call(
        functools.partial(_moe_up_body, tm=tm, de=de),
        grid_spec=pltpu.PrefetchScalarGridSpec(
            num_scalar_prefetch=7,
            grid=(n_tiles,),
            in_specs=[pl.BlockSpec(memory_space=pl.ANY),
                      pl.BlockSpec(memory_space=pl.ANY),
                      pl.BlockSpec((None, 1, de2), lambda i, te, *_: (te[i], 0, 0))],
            out_specs=pl.BlockSpec((tm, de), lambda i, *_: (i, 0)),
            scratch_shapes=[pltpu.VMEM((tm, d), F32), pltpu.VMEM((tm, d), BF16),
                            pltpu.VMEM((2, d, de2), BF16), pltpu.VMEM((2, d // W_CHUNKS, de2), F32),
                            pltpu.SemaphoreType.DMA(()), pltpu.SemaphoreType.DMA((2,))],
        ),
        out_shape=jax.ShapeDtypeStruct((n_rows, de), BF16),
        compiler_params=_cp(("arbitrary",)),
        name="moe_up",
    )(tile_expert, n_used, c_lo, c_hi, nxt, slot, src, hn, w_gu, b_gu.reshape(ne, 1, de2))
    return pl.pallas_call(
        _moe_down_body,
        grid_spec=pltpu.PrefetchScalarGridSpec(
            num_scalar_prefetch=2,
            grid=(n_tiles,),
            in_specs=[pl.BlockSpec((tm, de), lambda i, te, nu: (i, 0)),
                      pl.BlockSpec((None, de, d), lambda i, te, nu: (te[i], 0, 0)),
                      pl.BlockSpec((None, 1, d), lambda i, te, nu: (te[i], 0, 0))],
            out_specs=pl.BlockSpec((tm, d), lambda i, te, nu: (i, 0)),
            scratch_shapes=[pltpu.VMEM((de, d), BF16)],
        ),
        out_shape=jax.ShapeDtypeStruct((n_rows, d), F32),
        compiler_params=_cp(("arbitrary",)),
        name="moe_down",
    )(tile_expert, n_used, act, w_dn, b_dn.reshape(ne, 1, d))


def _combine_body(pos_ref, h_ref, cw_ref, g_ref, ys_ref, h2_ref, hn_ref, buf_ref, sem, *, tm):
    i = pl.program_id(0)

    last = pl.num_programs(0) - 1

    def start(tile, slot, r, k):
        p = pos_ref[tile * (tm * TOP_K) + r * TOP_K + k]
        pltpu.make_async_copy(ys_ref.at[pl.ds(p, 1)], buf_ref.at[slot, k, pl.ds(r, 1)], sem.at[slot]).start()

    def wait_slot(slot):
        for k in range(TOP_K):
            pltpu.make_async_copy(ys_ref.at[pl.ds(0, tm)], buf_ref.at[slot, k], sem.at[slot]).wait()

    @pl.when(i == 0)
    def _():
        def row(r, carry):
            for k in range(TOP_K):
                start(0, 0, r, k)
            return carry

        lax.fori_loop(0, tm, row, 0)

    slot = i % 2
    wait_slot(slot)
    nxt = jnp.minimum(i + 1, last)
    for r in range(tm):
        for k in range(TOP_K):
            start(nxt, 1 - slot, r, k)
    acc = h_ref[...]
    cw = cw_ref[...]
    for k in range(TOP_K):
        acc = acc + cw[:, k:k + 1] * buf_ref[slot, k]
    h2_ref[...] = acc
    hn_ref[...] = _rms(acc, g_ref[...]).astype(hn_ref.dtype)

    @pl.when(i == last)
    def _():
        wait_slot(1 - slot)


def moe_combine(h, cw, ys, pos_flat, norm_g, tm):
    t, d = h.shape
    grid_spec = pltpu.PrefetchScalarGridSpec(
        num_scalar_prefetch=1,
        grid=(t // tm,),
        in_specs=[pl.BlockSpec((tm, d), lambda i, pos: (i, 0)),
                  pl.BlockSpec((tm, LANE), lambda i, pos: (i, 0)),
                  pl.BlockSpec((1, d), lambda i, pos: (0, 0)),
                  pl.BlockSpec(memory_space=pl.ANY)],
        out_specs=[pl.BlockSpec((tm, d), lambda i, pos: (i, 0)),
                   pl.BlockSpec((tm, d), lambda i, pos: (i, 0))],
        scratch_shapes=[pltpu.VMEM((2, TOP_K, tm, d), F32), pltpu.SemaphoreType.DMA((2,))],
    )
    return pl.pallas_call(
        functools.partial(_combine_body, tm=tm),
        grid_spec=grid_spec,
        out_shape=[jax.ShapeDtypeStruct((t, d), F32), jax.ShapeDtypeStruct((t, d), BF16)],
        compiler_params=_cp(("arbitrary",)),
        name="moe_combine",
    )(pos_flat, h, cw, norm_g.reshape(1, -1).astype(F32), ys)


def _ple_body(a_ref, w_ref, p_ref, wp_ref, h_ref, o_ref):
    gate = _dot(a_ref[...], w_ref[...])
    pp = _dot(p_ref[...].astype(BF16), wp_ref[...])
    o_ref[...] = h_ref[...] + jax.nn.sigmoid(gate) * pp


def ple_gate(hn, w_gate, p, w_proj, h, *, tm=1024, tn=512):
    t, d = h.shape
    pd = p.shape[1]
    tm, tn = _tile(t, tm), _tile(d, tn)
    return pl.pallas_call(
        _ple_body,
        grid=(t // tm, d // tn),
        in_specs=[pl.BlockSpec((tm, d), lambda i, j: (i, 0)),
                  pl.BlockSpec((d, tn), lambda i, j: (0, j)),
                  pl.BlockSpec((tm, pd), lambda i, j: (i, 0)),
                  pl.BlockSpec((pd, tn), lambda i, j: (0, j)),
                  pl.BlockSpec((tm, tn), lambda i, j: (i, j))],
        out_specs=pl.BlockSpec((tm, tn), lambda i, j: (i, j)),
        out_shape=jax.ShapeDtypeStruct((t, d), F32),
        compiler_params=_cp(("parallel", "arbitrary")),
        name="ple_gate",
    )(hn, w_gate, p, w_proj, h)


def _split_w_in(w_in, qkw, vw, rank, qr, kvr, d_model):
    n_gla = 2 * qkw + 2 * vw
    off_gf = n_gla
    off_cq = off_gf + rank
    off_ckv = off_cq + qr
    off_kr = off_ckv + kvr
    off_gates = off_kr + MLA_ROPE
    kr = w_in[:, off_kr:off_gates]
    mid = jnp.concatenate([w_in[:, off_cq:off_kr], _pad_lane(kr), _pad_lane(_swap_halves(kr)),
                           _pad_lane(w_in[:, off_gf:off_cq])], axis=1)
    mid = jnp.pad(mid, ((0, 0), (0, -mid.shape[1] % 512)))
    lay = {"gq": 0, "gk": qkw, "gv": 2 * qkw, "gg": 2 * qkw + vw,
           "cq": 0, "ckv": qr, "krA": qr + kvr, "krB": qr + kvr + LANE, "gf": qr + kvr + 2 * LANE,
           "gate_gla": 0, "gate_mla": d_model}
    return (w_in[:, :n_gla].astype(BF16), mid.astype(BF16), w_in[:, off_gates:].astype(BF16)), lay


def _layer(h, p, w_in, gla_gate_up, gla_gate_bias, gla_out_norm, mla_q_norm, mla_kv_norm, mla_w_uq,
           mla_w_ukv, w_branch_gla, w_branch_mla, w_out, norm_mix, norm_moe, router_w, router_b,
           expert_w_gate_up, expert_b_gate_up, expert_w_down, expert_b_down, norm_ple, ple_gate_w,
           ple_proj_w, batch, seq):
    t, d = h.shape
    rank, qkw = gla_gate_up.shape
    vw = w_branch_gla.shape[0]
    qr, kvr = mla_q_norm.shape[0], mla_kv_norm.shape[0]
    (w_gla, w_mid, w_gates), lay = _split_w_in(w_in, qkw, vw, rank, qr, kvr, d)

    xn = rms_norm(h, norm_mix)
    proj_gla = matmul(xn, w_gla, out_dtype=BF16, name="in_proj_gla")
    proj_mid = matmul(xn, w_mid, out_dtype=BF16, name="in_proj_mid")
    proj_gates = matmul(xn, w_gates, out_dtype=BF16, name="in_proj_gates")
    o_gla = gla_mixer(proj_gla, proj_mid, lay, gla_gate_up, gla_gate_bias, gla_out_norm, batch, seq)
    o_mla = mla_mixer(proj_mid, lay, mla_q_norm, mla_kv_norm, mla_w_uq, mla_w_ukv, batch, seq)
    merged = merge_branches(o_gla, o_mla, w_branch_gla.astype(BF16), w_branch_mla.astype(BF16), proj_gates,
                            lay, d)
    h1 = matmul_residual(merged, w_out.astype(BF16), h, name="out_proj")

    ne = router_w.shape[1]
    tm_e = _tile(t, 256)
    hn, idx, rank_in, cw, cnt = moe_router(h1, norm_moe, router_w, router_b, tm_e)
    counts = cnt[0, :ne].astype(I32)
    tiles_e = -(-counts // tm_e)
    gend = jnp.cumsum(tiles_e) * tm_e
    gstart = gend - tiles_e * tm_e
    pos = (gstart[idx[:, :TOP_K]] + rank_in[:, :TOP_K]).reshape(-1)
    n_tiles = t * TOP_K // tm_e + ne
    tile_start = jnp.arange(n_tiles, dtype=I32) * tm_e
    tile_expert = jnp.minimum(jnp.sum((gend[None, :] <= tile_start[:, None]).astype(I32), axis=1), ne - 1)
    n_used = (gend[-1:] // tm_e).astype(I32)
    sched = _weight_stream_schedule(tile_expert, n_used, tiles_e, gstart // tm_e)
    src = moe_sources(pos, n_tiles * tm_e)
    ys = moe_experts(hn, src, tile_expert, n_used, sched, expert_w_gate_up, expert_b_gate_up,
                     expert_w_down, expert_b_down, tm_e)
    h2, hn2 = moe_combine(h1, cw, ys, pos, norm_ple, _tile(t, 128))

    return ple_gate(hn2, ple_gate_w.astype(BF16), p, ple_proj_w.astype(BF16), h2)


def kernel(x, p, w_in, gla_gate_up, gla_gate_bias, gla_out_norm, mla_q_norm, mla_kv_norm, mla_w_uq, mla_w_ukv, w_branch_gla, w_branch_mla, w_out, norm_mix, norm_moe, router_w, router_b, expert_w_gate_up, expert_b_gate_up, expert_w_down, expert_b_down, norm_ple, ple_gate_w, ple_proj_w, final_norm):
    batch, seq, d = x.shape
    h = x.reshape(batch * seq, d)
    for i in range(p.shape[0]):
        h = _layer(h, p[i].reshape(batch * seq, -1), w_in[i], gla_gate_up[i], gla_gate_bias[i],
                   gla_out_norm[i], mla_q_norm[i], mla_kv_norm[i], mla_w_uq[i], mla_w_ukv[i],
                   w_branch_gla[i], w_branch_mla[i], w_out[i], norm_mix[i], norm_moe[i], router_w[i],
                   router_b[i], expert_w_gate_up[i], expert_b_gate_up[i], expert_w_down[i],
                   expert_b_down[i], norm_ple[i], ple_gate_w[i], ple_proj_w[i], batch, seq)
    return rms_norm(h, final_norm, out_dtype=F32).reshape(batch, seq, d)
```

```python
import functools
import math

import jax
import jax.numpy as jnp
from jax import lax
from jax.experimental import pallas as pl
from jax.experimental.pallas import tpu as pltpu

F32 = jnp.float32
BF16 = jnp.bfloat16
U32 = jnp.uint32
I32 = jnp.int32

CHUNK = 64
EPS = 1e-6
GLA_HEADS = 8
GLA_TAU = 16.0
MLA_HEADS = 16
MLA_NOPE = 128
MLA_ROPE = 64
MLA_V = 128
ROPE_THETA = 10000.0
TOP_K = 4
SWIGLU_LIMIT = 7.0
SWIGLU_ALPHA = 1.702

LANE = 128
SUBLANE = 8
VMEM_LIMIT = 56 * 2 ** 20
NEG_BIG = -1e30


def _tile(n, pref):
    if n <= pref:
        return n
    t = pref
    while t >= 8:
        if n % t == 0 and t % 8 == 0:
            return t
        t -= 8
    return n


def _cp(sem):
    return pltpu.CompilerParams(dimension_semantics=sem, vmem_limit_bytes=VMEM_LIMIT)


def _dot(a, b):
    return jnp.dot(a, b, preferred_element_type=F32)


def _dot_nt(a, b):
    return lax.dot_general(a, b, (((1,), (1,)), ((), ())), preferred_element_type=F32)


def _dot_tn(a, b):
    return lax.dot_general(a, b, (((0,), (0,)), ((), ())), preferred_element_type=F32)


def _rms(x, g):
    return x * lax.rsqrt(jnp.mean(x * x, axis=-1, keepdims=True) + EPS) * g


def _rms_body(x_ref, g_ref, o_ref):
    o_ref[...] = _rms(x_ref[...].astype(F32), g_ref[...]).astype(o_ref.dtype)


def rms_norm(x, g, *, col_block=0, width=None, out_dtype=BF16, tm=512):
    m = x.shape[0]
    width = width or x.shape[1]
    tm = _tile(m, tm)
    return pl.pallas_call(
        _rms_body,
        grid=(m // tm,),
        in_specs=[pl.BlockSpec((tm, width), lambda i: (i, col_block)),
                  pl.BlockSpec((1, width), lambda i: (0, 0))],
        out_specs=pl.BlockSpec((tm, width), lambda i: (i, 0)),
        out_shape=jax.ShapeDtypeStruct((m, width), out_dtype),
        compiler_params=_cp(("parallel",)),
        name="rms_norm",
    )(x, g.reshape(1, -1).astype(F32))


def _mm_body(a_ref, w_ref, o_ref):
    o_ref[...] = _dot(a_ref[...], w_ref[...]).astype(o_ref.dtype)


def matmul(a, w, *, out_dtype, tm=1024, tn=1024, name="matmul"):
    m, k = a.shape
    n = w.shape[1]
    tm, tn = _tile(m, tm), _tile(n, tn)
    return pl.pallas_call(
        _mm_body,
        grid=(m // tm, n // tn),
        in_specs=[pl.BlockSpec((tm, k), lambda i, j: (i, 0)),
                  pl.BlockSpec((k, tn), lambda i, j: (0, j))],
        out_specs=pl.BlockSpec((tm, tn), lambda i, j: (i, j)),
        out_shape=jax.ShapeDtypeStruct((m, n), out_dtype),
        compiler_params=_cp(("parallel", "arbitrary")),
        name=name,
    )(a, w)


def _gla_body(q_ref, k_ref, v_ref, gg_ref, gf_ref, gup_ref, gb_ref, on_ref, tri_ref, blk_ref,
              o_ref, *scratch, nchunk, scale, hg, dk, dv):
    st_refs, ob_refs = scratch[:hg], scratch[hg:]

    @pl.when(pl.program_id(2) == 0)
    def _():
        for st_ref in st_refs:
            st_ref[...] = jnp.zeros_like(st_ref)

    logit = _dot(gf_ref[...], gup_ref[...]) + gb_ref[...]
    la = (jnp.minimum(logit, 0.0) - jnp.log(1.0 + jnp.exp(-jnp.abs(logit)))) * (1.0 / GLA_TAU)
    la_hi = la.astype(BF16)
    la_lo = (la - la_hi.astype(F32)).astype(BF16)
    tri = tri_ref[...]
    blk = blk_ref[...]
    b = _dot(tri, la_hi) + _dot(tri, la_lo)
    bl = _dot(blk, la_hi) + _dot(blk, la_lo)
    q = q_ref[...].astype(F32)
    k = k_ref[...].astype(F32)
    qd = (q * scale * jnp.exp(b)).astype(BF16)
    kd = (k * jnp.exp(-b)).astype(BF16)
    ke = (k * jnp.exp(bl - b)).astype(BF16)
    dec = jnp.exp(bl)
    row = lax.broadcasted_iota(I32, (CHUNK, CHUNK), 0)
    col = lax.broadcasted_iota(I32, (CHUNK, CHUNK), 1)
    causal = row >= col
    for c in range(nchunk):
        lo = c * CHUNK
        for h in range(hg):
            ks, vs = slice(h * dk, (h + 1) * dk), slice(h * dv, (h + 1) * dv)
            qc = qd[lo:lo + CHUNK, ks]
            kc = kd[lo:lo + CHUNK, ks]
            kec = ke[lo:lo + CHUNK, ks]
            vc = v_ref[lo:lo + CHUNK, vs]
            s = jnp.where(causal, _dot_nt(qc, kc), 0.0).astype(BF16)
            st = st_refs[h][...]
            ob_refs[h][lo:lo + CHUNK, :] = _dot(s, vc) + _dot_nt(qc, st.astype(BF16))
            st_refs[h][...] = st * dec[lo:lo + 1, ks] + _dot_tn(vc, kec)
    for h in range(hg):
        vs = slice(h * dv, (h + 1) * dv)
        o = _rms(ob_refs[h][...], on_ref[...])
        g = gg_ref[:, vs].astype(F32)
        o_ref[:, vs] = (o * (g * jax.nn.sigmoid(g))).astype(o_ref.dtype)


def gla_mixer(proj, proj_mid, lay, gate_up, gate_bias, out_norm, batch, seq):
    heads = GLA_HEADS
    dk = gate_up.shape[1] // heads
    dv = out_norm.shape[0]
    rank = gate_up.shape[0]
    lb = _tile(seq, 512)
    nchunk = lb // CHUNK
    nsb = seq // lb
    t = batch * seq
    gup = jnp.zeros((LANE, heads * dk), F32).at[:rank].set(gate_up).astype(BF16)
    r = jnp.arange(lb)
    same = (r[:, None] // CHUNK) == (r[None, :] // CHUNK)
    tri = (same & (r[None, :] <= r[:, None])).astype(BF16)
    blk = same.astype(BF16)
    hg = 2 if heads % 2 == 0 else 1
    wk, wv = hg * dk, hg * dv
    qb, kb = lay["gq"] // wk, lay["gk"] // wk
    vb, gb = lay["gv"] // wv, lay["gg"] // wv
    fb = lay["gf"] // LANE
    rowmap = lambda b, h, s: b * nsb + s
    return pl.pallas_call(
        functools.partial(_gla_body, nchunk=nchunk, scale=dk ** -0.5, hg=hg, dk=dk, dv=dv),
        grid=(batch, heads // hg, nsb),
        in_specs=[
            pl.BlockSpec((lb, wk), lambda b, h, s: (rowmap(b, h, s), qb + h)),
            pl.BlockSpec((lb, wk), lambda b, h, s: (rowmap(b, h, s), kb + h)),
            pl.BlockSpec((lb, wv), lambda b, h, s: (rowmap(b, h, s), vb + h)),
            pl.BlockSpec((lb, wv), lambda b, h, s: (rowmap(b, h, s), gb + h)),
            pl.BlockSpec((lb, LANE), lambda b, h, s: (rowmap(b, h, s), fb)),
            pl.BlockSpec((LANE, wk), lambda b, h, s: (0, h)),
            pl.BlockSpec((1, wk), lambda b, h, s: (0, h)),
            pl.BlockSpec((1, dv), lambda b, h, s: (0, 0)),
            pl.BlockSpec((lb, lb), lambda b, h, s: (0, 0)),
            pl.BlockSpec((lb, lb), lambda b, h, s: (0, 0)),
        ],
        out_specs=pl.BlockSpec((lb, wv), lambda b, h, s: (rowmap(b, h, s), h)),
        out_shape=jax.ShapeDtypeStruct((t, heads * dv), BF16),
        scratch_shapes=[pltpu.VMEM((dv, dk), F32)] * hg + [pltpu.VMEM((lb, dv), F32)] * hg,
        compiler_params=_cp(("parallel", "parallel", "arbitrary")),
        name="gla_mixer",
    )(proj, proj, proj, proj, proj_mid, gup, gate_bias.reshape(1, -1).astype(F32),
      out_norm.reshape(1, -1).astype(F32), tri, blk)


def _q_up_body(a_ref, w_ref, c_ref, s_ref, o_ref, *, hpt, scale):
    acc = _dot(a_ref[...], w_ref[...])
    c = c_ref[...]
    s = s_ref[...]
    for h in range(hpt):
        src = h * 3 * LANE
        dst = h * 2 * LANE
        o_ref[:, dst:dst + LANE] = (acc[:, src:src + LANE] * scale).astype(o_ref.dtype)
        pe = acc[:, src + LANE:src + 2 * LANE] * c + acc[:, src + 2 * LANE:src + 3 * LANE] * s
        o_ref[:, dst + LANE:dst + 2 * LANE] = (pe * scale).astype(o_ref.dtype)


def _k_up_body(a_ref, w_ref, kra_ref, krb_ref, c_ref, s_ref, o_ref, *, hpt):
    acc = _dot(a_ref[...], w_ref[...])
    kpe = (kra_ref[...].astype(F32) * c_ref[...] + krb_ref[...].astype(F32) * s_ref[...]).astype(o_ref.dtype)
    for h in range(hpt):
        o_ref[:, 2 * h * LANE:(2 * h + 1) * LANE] = acc[:, h * LANE:(h + 1) * LANE].astype(o_ref.dtype)
        o_ref[:, (2 * h + 1) * LANE:(2 * h + 2) * LANE] = kpe


def _v_up_body(a_ref, w_ref, o_ref, *, hpt):
    acc = _dot(a_ref[...], w_ref[...])
    lane = lax.broadcasted_iota(I32, (acc.shape[0], LANE), 1)
    ones_col = (lane == 0).astype(o_ref.dtype)
    for h in range(hpt):
        o_ref[:, 2 * h * LANE:(2 * h + 1) * LANE] = acc[:, h * LANE:(h + 1) * LANE].astype(o_ref.dtype)
        o_ref[:, (2 * h + 1) * LANE:(2 * h + 2) * LANE] = ones_col


def _swap_halves(w):
    half = w.shape[-1] // 2
    return jnp.concatenate([w[..., half:], w[..., :half]], axis=-1)


def _pad_lane(w):
    pad = LANE - w.shape[-1]
    return jnp.pad(w, [(0, 0)] * (w.ndim - 1) + [(0, pad)])


def mla_mixer(proj, lay, q_norm, kv_norm, w_uq, w_ukv, batch, seq):
    heads = MLA_HEADS
    t = batch * seq
    qr, kvr = q_norm.shape[0], kv_norm.shape[0]
    assert MLA_NOPE == LANE and MLA_V == LANE and MLA_ROPE <= LANE
    scale = (MLA_NOPE + MLA_ROPE) ** -0.5 * math.log2(math.e)
    tm = _tile(seq, 1024)
    nrb = seq // tm
    hpt = 4 if heads % 4 == 0 else 1

    pos = jnp.arange(seq, dtype=F32)
    inv_freq = ROPE_THETA ** (-jnp.arange(0, MLA_ROPE, 2, dtype=F32) / MLA_ROPE)
    ang = pos[:, None] * inv_freq[None, :]
    cos, sin = jnp.cos(ang), jnp.sin(ang)
    ctab = _pad_lane(jnp.concatenate([cos, cos], axis=-1))
    stab = _pad_lane(jnp.concatenate([-sin, sin], axis=-1))

    cqn = rms_norm(proj, q_norm, col_block=lay["cq"] // qr, width=qr)
    ckvn = rms_norm(proj, kv_norm, col_block=lay["ckv"] // kvr, width=kvr)

    wq = w_uq.reshape(qr, heads, MLA_NOPE + MLA_ROPE)
    wq_pe = wq[..., MLA_NOPE:]
    wq_r = jnp.concatenate([wq[..., :MLA_NOPE], _pad_lane(wq_pe), _pad_lane(_swap_halves(wq_pe))],
                           axis=-1).reshape(qr, heads * 3 * LANE).astype(BF16)
    q_cat = pl.pallas_call(
        functools.partial(_q_up_body, hpt=hpt, scale=scale),
        grid=(t // tm, heads // hpt),
        in_specs=[pl.BlockSpec((tm, qr), lambda i, j: (i, 0)),
                  pl.BlockSpec((qr, hpt * 3 * LANE), lambda i, j: (0, j)),
                  pl.BlockSpec((tm, LANE), lambda i, j: (i % nrb, 0)),
                  pl.BlockSpec((tm, LANE), lambda i, j: (i % nrb, 0))],
        out_specs=pl.BlockSpec((tm, hpt * 2 * LANE), lambda i, j: (i, j)),
        out_shape=jax.ShapeDtypeStruct((t, heads * 2 * LANE), BF16),
        compiler_params=_cp(("parallel", "arbitrary")),
        name="mla_q_up",
    )(cqn, wq_r, ctab, stab)

    wkv = w_ukv.reshape(kvr, heads, MLA_NOPE + MLA_V)
    wk = wkv[..., :MLA_NOPE].reshape(kvr, heads * MLA_NOPE).astype(BF16)
    wv = wkv[..., MLA_NOPE:].reshape(kvr, heads * MLA_V).astype(BF16)
    ka, kb = lay["krA"] // LANE, lay["krB"] // LANE
    k_cat = pl.pallas_call(
        functools.partial(_k_up_body, hpt=hpt),
        grid=(t // tm, heads // hpt),
        in_specs=[pl.BlockSpec((tm, kvr), lambda i, j: (i, 0)),
                  pl.BlockSpec((kvr, hpt * LANE), lambda i, j: (0, j)),
                  pl.BlockSpec((tm, LANE), lambda i, j: (i, ka)),
                  pl.BlockSpec((tm, LANE), lambda i, j: (i, kb)),
                  pl.BlockSpec((tm, LANE), lambda i, j: (i % nrb, 0)),
                  pl.BlockSpec((tm, LANE), lambda i, j: (i % nrb, 0))],
        out_specs=pl.BlockSpec((tm, hpt * 2 * LANE), lambda i, j: (i, j)),
        out_shape=jax.ShapeDtypeStruct((t, heads * 2 * LANE), BF16),
        compiler_params=_cp(("parallel", "arbitrary")),
        name="mla_k_up",
    )(ckvn, wk, proj, proj, ctab, stab)
    v_ext = pl.pallas_call(
        functools.partial(_v_up_body, hpt=hpt),
        grid=(t // tm, heads // hpt),
        in_specs=[pl.BlockSpec((tm, kvr), lambda i, j: (i, 0)),
                  pl.BlockSpec((kvr, hpt * LANE), lambda i, j: (0, j))],
        out_specs=pl.BlockSpec((tm, hpt * 2 * LANE), lambda i, j: (i, j)),
        out_shape=jax.ShapeDtypeStruct((t, heads * 2 * LANE), BF16),
        compiler_params=_cp(("parallel", "arbitrary")),
        name="mla_v_up",
    )(ckvn, wv)
    return mla_attention(q_cat, k_cat, v_ext, batch, seq)


def _mla_attn_body(q_ref, k_ref, v_ref, o_ref, m_ref, acc_ref, *, tq, tk, hg):
    qi = pl.program_id(2)
    m_ref[...] = jnp.full_like(m_ref, NEG_BIG)
    acc_ref[...] = jnp.zeros_like(acc_ref)
    w = 2 * LANE

    def block(ki, masked):
        ks = pl.multiple_of(ki * tk, tk)
        for g in range(hg):
            s = _dot_nt(q_ref[:, g * w:(g + 1) * w], k_ref[pl.ds(ks, tk), g * w:(g + 1) * w])
            if masked:
                qc = (qi * tq + lax.broadcasted_iota(I32, (tq, tk), 0)) // CHUNK
                kc = (ki * tk + lax.broadcasted_iota(I32, (tq, tk), 1)) // CHUNK
                s = jnp.where(kc <= qc, s, NEG_BIG)
            m_prev = m_ref[g]
            m_new = jnp.maximum(m_prev, jnp.max(s, axis=-1, keepdims=True))
            p = jnp.exp2(s - m_new).astype(BF16)
            acc_ref[g] = (jnp.exp2(m_prev - m_new) * acc_ref[g]
                          + _dot(p, v_ref[pl.ds(ks, tk), g * w:(g + 1) * w]))
            m_ref[g] = m_new

    def body(ki, carry):
        block(ki, False)
        return carry

    n_full = qi * (tq // tk)
    lax.fori_loop(0, n_full, body, 0)
    for dblk in range(tq // tk):
        block(n_full + dblk, True)
    for g in range(hg):
        acc = acc_ref[g]
        o_ref[:, g * LANE:(g + 1) * LANE] = (acc[:, :LANE] / acc[:, LANE:LANE + 1]).astype(o_ref.dtype)


def mla_attention(q_cat, k_cat, v_ext, batch, seq):
    heads = MLA_HEADS
    t = batch * seq
    tq = _tile(seq, 512)
    tk = tq
    nq = seq // tq
    hg = 4 if heads % 4 == 0 else 1
    w = 2 * LANE
    return pl.pallas_call(
        functools.partial(_mla_attn_body, tq=tq, tk=tk, hg=hg),
        grid=(batch, heads // hg, nq),
        in_specs=[pl.BlockSpec((tq, hg * w), lambda b, h, i: (b * nq + i, h)),
                  pl.BlockSpec((seq, hg * w), lambda b, h, i: (b, h)),
                  pl.BlockSpec((seq, hg * w), lambda b, h, i: (b, h))],
        out_specs=pl.BlockSpec((tq, hg * LANE), lambda b, h, i: (b * nq + i, h)),
        out_shape=jax.ShapeDtypeStruct((t, heads * MLA_V), BF16),
        scratch_shapes=[pltpu.VMEM((hg, tq, 1), F32), pltpu.VMEM((hg, tq, w), F32)],
        compiler_params=_cp(("parallel", "parallel", "arbitrary")),
        name="mla_attention",
    )(q_cat, k_cat, v_ext)


def _merge_body(a1_ref, w1_ref, a2_ref, w2_ref, g1_ref, g2_ref, o_ref):
    y1 = _dot(a1_ref[...], w1_ref[...])
    y2 = _dot(a2_ref[...], w2_ref[...])
    o_ref[...] = (jax.nn.sigmoid(g1_ref[...].astype(F32)) * y1
                  + jax.nn.sigmoid(g2_ref[...].astype(F32)) * y2).astype(o_ref.dtype)


def merge_branches(o_gla, o_mla, w_gla, w_mla, proj, lay, d_model):
    t = o_gla.shape[0]
    tm = _tile(t, 1024)
    tn = _tile(d_model, 512)
    g1, g2 = lay["gate_gla"] // tn, lay["gate_mla"] // tn
    k1, k2 = o_gla.shape[1], o_mla.shape[1]
    return pl.pallas_call(
        _merge_body,
        grid=(t // tm, d_model // tn),
        in_specs=[pl.BlockSpec((tm, k1), lambda i, j: (i, 0)),
                  pl.BlockSpec((k1, tn), lambda i, j: (0, j)),
                  pl.BlockSpec((tm, k2), lambda i, j: (i, 0)),
                  pl.BlockSpec((k2, tn), lambda i, j: (0, j)),
                  pl.BlockSpec((tm, tn), lambda i, j: (i, g1 + j)),
                  pl.BlockSpec((tm, tn), lambda i, j: (i, g2 + j))],
        out_specs=pl.BlockSpec((tm, tn), lambda i, j: (i, j)),
        out_shape=jax.ShapeDtypeStruct((t, d_model), BF16),
        compiler_params=_cp(("parallel", "arbitrary")),
        name="merge_branches",
    )(o_gla, w_gla, o_mla, w_mla, proj, proj)


def _mm_res_body(a_ref, w_ref, r_ref, o_ref):
    o_ref[...] = r_ref[...] + _dot(a_ref[...], w_ref[...])


def matmul_residual(a, w, res, *, tm=1024, tn=512, name="matmul_residual"):
    m, k = a.shape
    n = w.shape[1]
    tm, tn = _tile(m, tm), _tile(n, tn)
    return pl.pallas_call(
        _mm_res_body,
        grid=(m // tm, n // tn),
        in_specs=[pl.BlockSpec((tm, k), lambda i, j: (i, 0)),
                  pl.BlockSpec((k, tn), lambda i, j: (0, j)),
                  pl.BlockSpec((tm, tn), lambda i, j: (i, j))],
        out_specs=pl.BlockSpec((tm, tn), lambda i, j: (i, j)),
        out_shape=jax.ShapeDtypeStruct((m, n), F32),
        compiler_params=_cp(("parallel", "arbitrary")),
        name=name,
    )(a, w, res)


def _router_body(h_ref, g_ref, rwh_ref, rwl_ref, rb_ref, tri_ref, hn_ref, idx_ref, rank_ref, cw_ref, cnt_ref,
                 carry_ref):
    @pl.when(pl.program_id(0) == 0)
    def _():
        carry_ref[...] = jnp.zeros_like(carry_ref)

    hn = _rms(h_ref[...], g_ref[...])
    hn_ref[...] = hn
    hn_hi = hn.astype(BF16)
    hn_lo = (hn - hn_hi.astype(F32)).astype(BF16)
    logits = (_dot(hn_hi, rwh_ref[...]) + _dot(hn_hi, rwl_ref[...]) + _dot(hn_lo, rwh_ref[...])) + rb_ref[...]
    tm = logits.shape[0]
    lane = lax.broadcasted_iota(I32, (tm, LANE), 1)
    work = logits
    vals, sels, idxs = [], [], []
    for _ in range(TOP_K):
        mx = jnp.max(work, axis=-1, keepdims=True)
        ix = jnp.min(jnp.where(work == mx, lane, LANE), axis=-1, keepdims=True)
        sel = lane == ix
        work = jnp.where(sel, -jnp.inf, work)
        vals.append(mx)
        sels.append(sel)
        idxs.append(ix)
    es = [jnp.exp(v - vals[0]) for v in vals]
    den = es[0]
    for e in es[1:]:
        den = den + e
    hot = jnp.zeros((tm, LANE), F32)
    for sel in sels:
        hot = hot + sel.astype(F32)
    carry = carry_ref[0:1, :]
    rank = _dot(tri_ref[...], hot.astype(BF16)) + carry
    idx_out = jnp.zeros((tm, LANE), I32)
    rank_out = jnp.zeros((tm, LANE), F32)
    cw_out = jnp.zeros((tm, LANE), F32)
    for k in range(TOP_K):
        rk = jnp.sum(jnp.where(sels[k], rank, 0.0), axis=-1, keepdims=True)
        idx_out = jnp.where(lane == k, idxs[k], idx_out)
        rank_out = jnp.where(lane == k, rk, rank_out)
        cw_out = jnp.where(lane == k, es[k] / den, cw_out)
    idx_ref[...] = idx_out
    rank_ref[...] = rank_out.astype(I32)
    cw_ref[...] = cw_out
    new_carry = carry + jnp.sum(hot, axis=0, keepdims=True)
    carry_ref[...] = jnp.broadcast_to(new_carry, carry_ref.shape)
    cnt_ref[...] = jnp.broadcast_to(new_carry, cnt_ref.shape)


def moe_router(h, norm_g, router_w, router_b, tm):
    t, d = h.shape
    ne = router_w.shape[1]
    assert ne <= LANE and d % (2 * LANE) == 0
    rw = jnp.zeros((d, LANE), F32).at[:, :ne].set(router_w)
    rw_hi = rw.astype(BF16)
    rw_lo = (rw - rw_hi.astype(F32)).astype(BF16)
    rb = jnp.full((1, LANE), NEG_BIG, F32).at[0, :ne].set(router_b)
    r = jnp.arange(tm)
    tri = (r[None, :] < r[:, None]).astype(BF16)
    row = lambda i: (i, 0)
    const = lambda i: (0, 0)
    return pl.pallas_call(
        _router_body,
        grid=(t // tm,),
        in_specs=[pl.BlockSpec((tm, d), row), pl.BlockSpec((1, d), const),
                  pl.BlockSpec((d, LANE), const), pl.BlockSpec((d, LANE), const),
                  pl.BlockSpec((1, LANE), const), pl.BlockSpec((tm, tm), const)],
        out_specs=[pl.BlockSpec((tm, d), row), pl.BlockSpec((tm, LANE), row),
                   pl.BlockSpec((tm, LANE), row), pl.BlockSpec((tm, LANE), row),
                   pl.BlockSpec((8, LANE), const)],
        out_shape=[jax.ShapeDtypeStruct((t, d), F32), jax.ShapeDtypeStruct((t, LANE), I32),
                   jax.ShapeDtypeStruct((t, LANE), I32), jax.ShapeDtypeStruct((t, LANE), F32),
                   jax.ShapeDtypeStruct((8, LANE), F32)],
        scratch_shapes=[pltpu.VMEM((8, LANE), F32)],
        compiler_params=_cp(("arbitrary",)),
        name="moe_router",
    )(h, norm_g.reshape(1, -1).astype(F32), rw_hi, rw_lo, rb, tri)


def _sources_body(pos_ref, src_ref, *, nsteps):
    i = pl.program_id(0)
    rows_per = src_ref.shape[0] // nsteps
    toks_per = pos_ref.shape[0] // TOP_K // nsteps

    @pl.when(i < nsteps)
    def _():
        def clear(r, carry):
            src_ref[i * rows_per + r] = 0
            return carry

        lax.fori_loop(0, rows_per, clear, 0, unroll=8)

    @pl.when(i >= nsteps)
    def _():
        t0 = (i - nsteps) * toks_per

        def put(j, carry):
            for k in range(TOP_K):
                src_ref[pos_ref[(t0 + j) * TOP_K + k]] = t0 + j
            return carry

        lax.fori_loop(0, toks_per, put, 0, unroll=2)


def moe_sources(pos_flat, n_rows):
    n_tok = pos_flat.shape[0] // TOP_K
    nsteps = 64 if (n_rows % 64 == 0 and n_tok % 64 == 0) else 1
    return pl.pallas_call(
        functools.partial(_sources_body, nsteps=nsteps),
        grid=(2 * nsteps,),
        in_specs=[pl.BlockSpec(memory_space=pltpu.SMEM)],
        out_specs=pl.BlockSpec(memory_space=pltpu.SMEM),
        out_shape=jax.ShapeDtypeStruct((n_rows,), I32),
        compiler_params=_cp(("arbitrary",)),
        name="moe_sources",
    )(pos_flat)


W_CHUNKS = 16


def _moe_up_body(te_ref, nu_ref, clo_ref, chi_ref, nxt_ref, slot_ref, src_ref, hn_ref, w_ref, b_ref, o_ref,
                 xf_ref, xb_ref, wb_ref, st_ref, gsem, wsem, *, tm, de):
    i = pl.program_id(0)
    ch = w_ref.shape[1] // W_CHUNKS

    def w_copy(e, c, buf):
        return pltpu.make_async_copy(w_ref.at[e, pl.ds(c * ch, ch)], st_ref.at[buf], wsem.at[buf])

    def convert(c, buf, slot):
        wb_ref[slot, pl.ds(pl.multiple_of(c * ch, ch), ch), :] = st_ref[buf].astype(BF16)

    def gather(tile):
        base = tile * tm

        def rows8(g, carry):
            r0 = pl.multiple_of(g * SUBLANE, SUBLANE)
            for j in range(SUBLANE):
                pltpu.make_async_copy(hn_ref.at[pl.ds(src_ref[base + r0 + j], 1)],
                                      xf_ref.at[pl.ds(r0 + j, 1)], gsem).start()
            return carry

        lax.fori_loop(0, tm // SUBLANE, rows8, 0)

    @pl.when(i == 0)
    def _():
        e0, s0 = te_ref[0], slot_ref[0]
        w_copy(e0, 0, 0).start()

        def load(c, carry):
            @pl.when(c + 1 < W_CHUNKS)
            def _():
                w_copy(e0, c + 1, (c + 1) % 2).start()

            w_copy(e0, c, c % 2).wait()
            convert(c, c % 2, s0)
            return carry

        lax.fori_loop(0, W_CHUNKS, load, 0)
        gather(0)

    @pl.when(i < nu_ref[0])
    def _():
        pltpu.make_async_copy(hn_ref.at[pl.ds(0, tm)], xf_ref, gsem).wait()
        xb_ref[...] = xf_ref[...].astype(BF16)

        @pl.when(i + 1 < nu_ref[0])
        def _():
            gather(i + 1)

        e_n, s_n = nxt_ref[i], 1 - slot_ref[i]
        lo, hi = clo_ref[i], chi_ref[i]

        @pl.when((lo == 0) & (hi > 0))
        def _():
            w_copy(e_n, 0, 0).start()
            w_copy(e_n, 1, 1).start()

        def chunk(c, carry):
            w_copy(e_n, c, c % 2).wait()
            convert(c, c % 2, s_n)

            @pl.when(c + 2 < W_CHUNKS)
            def _():
                w_copy(e_n, c + 2, c % 2).start()

            return carry

        lax.fori_loop(lo, hi, chunk, 0)

        gu = _dot(xb_ref[...], wb_ref[slot_ref[i]]) + b_ref[...]
        gate = jnp.minimum(gu[:, :de], SWIGLU_LIMIT)
        up = jnp.clip(gu[:, de:], -SWIGLU_LIMIT, SWIGLU_LIMIT)
        o_ref[...] = ((up + 1.0) * (gate * jax.nn.sigmoid(SWIGLU_ALPHA * gate))).astype(o_ref.dtype)

    @pl.when(i >= nu_ref[0])
    def _():
        o_ref[...] = jnp.zeros_like(o_ref)


def _moe_down_body(te_ref, nu_ref, a_ref, w_ref, b_ref, o_ref, wb_ref):
    i = pl.program_id(0)

    @pl.when(i < nu_ref[0])
    def _():
        @pl.when((i == 0) | (te_ref[i] != te_ref[jnp.maximum(i - 1, 0)]))
        def _():
            wb_ref[...] = w_ref[...].astype(BF16)

        o_ref[...] = _dot(a_ref[...], wb_ref[...]) + b_ref[...]

    @pl.when(i >= nu_ref[0])
    def _():
        o_ref[...] = jnp.zeros_like(o_ref)


def _weight_stream_schedule(tile_expert, n_used, tiles_per_expert, group_start_tile):
    ne = tiles_per_expert.shape[0]
    n_tiles = tile_expert.shape[0]
    nonempty = tiles_per_expert > 0
    ordinal = jnp.cumsum(nonempty.astype(I32)) - 1
    cand = jnp.where(nonempty, jnp.arange(ne, dtype=I32), ne)
    next_ge = lax.cummin(cand, axis=0, reverse=True)
    next_gt = jnp.concatenate([next_ge[1:], jnp.full((1,), ne, I32)])
    tile = jnp.arange(n_tiles, dtype=I32)
    k_in = tile - group_start_tile[tile_expert]
    n_in = jnp.maximum(tiles_per_expert[tile_expert], 1)
    nxt = next_gt[tile_expert]
    live = (nxt < ne) & (tile < n_used[0])
    c_lo = jnp.where(live, k_in * W_CHUNKS // n_in, 0).astype(I32)
    c_hi = jnp.where(live, (k_in + 1) * W_CHUNKS // n_in, 0).astype(I32)
    return c_lo, c_hi, jnp.where(live, nxt, 0).astype(I32), (ordinal[tile_expert] % 2).astype(I32)


def moe_experts(hn, src, tile_expert, n_used, sched, w_gu, b_gu, w_dn, b_dn, tm):
    d = hn.shape[1]
    n_rows = src.shape[0]
    ne, _, de2 = w_gu.shape
    de = de2 // 2
    n_tiles = n_rows // tm
    assert d % W_CHUNKS == 0 and W_CHUNKS >= 2
    c_lo, c_hi, nxt, slot = sched
    act = pl.pallas_call(
        functools.partial(_moe_up_body, tm=tm, de=de),
        grid_spec=pltpu.PrefetchScalarGridSpec(
            num_scalar_prefetch=7,
            grid=(n_tiles,),
            in_specs=[pl.BlockSpec(memory_space=pl.ANY),
                      pl.BlockSpec(memory_space=pl.ANY),
                      pl.BlockSpec((None, 1, de2), lambda i, te, *_: (te[i], 0, 0))],
            out_specs=pl.BlockSpec((tm, de), lambda i, *_: (i, 0)),
            scratch_shapes=[pltpu.VMEM((tm, d), F32), pltpu.VMEM((tm, d), BF16),
                            pltpu.VMEM((2, d, de2), BF16), pltpu.VMEM((2, d // W_CHUNKS, de2), F32),
                            pltpu.SemaphoreType.DMA(()), pltpu.SemaphoreType.DMA((2,))],
        ),
        out_shape=jax.ShapeDtypeStruct((n_rows, de), BF16),
        compiler_params=_cp(("arbitrary",)),
        name="moe_up",
    )(tile_expert, n_used, c_lo, c_hi, nxt, slot, src, hn, w_gu, b_gu.reshape(ne, 1, de2))
    return pl.pallas_call(
        _moe_down_body,
        grid_spec=pltpu.PrefetchScalarGridSpec(
            num_scalar_prefetch=2,
            grid=(n_tiles,),
            in_specs=[pl.BlockSpec((tm, de), lambda i, te, nu: (i, 0)),
                      pl.BlockSpec((None, de, d), lambda i, te, nu: (te[i], 0, 0)),
                      pl.BlockSpec((None, 1, d), lambda i, te, nu: (te[i], 0, 0))],
            out_specs=pl.BlockSpec((tm, d), lambda i, te, nu: (i, 0)),
            scratch_shapes=[pltpu.VMEM((de, d), BF16)],
        ),
        out_shape=jax.ShapeDtypeStruct((n_rows, d), F32),
        compiler_params=_cp(("arbitrary",)),
        name="moe_down",
    )(tile_expert, n_used, act, w_dn, b_dn.reshape(ne, 1, d))


def _combine_body(pos_ref, h_ref, cw_ref, g_ref, ys_ref, h2_ref, hn_ref, buf_ref, sem, *, tm):
    i = pl.program_id(0)

    def issue(tile, slot):
        base = tile * (tm * TOP_K)

        def row(r, carry):
            for k in range(TOP_K):
                p = pos_ref[base + r * TOP_K + k]
                pltpu.make_async_copy(ys_ref.at[pl.ds(p, 1)], buf_ref.at[slot, k, pl.ds(r, 1)],
                                      sem.at[slot]).start()
            return carry

        lax.fori_loop(0, tm, row, 0, unroll=2)

    @pl.when(i == 0)
    def _():
        issue(0, 0)

    @pl.when(i + 1 < pl.num_programs(0))
    def _():
        issue(i + 1, (i + 1) % 2)

    slot = i % 2
    for k in range(TOP_K):
        pltpu.make_async_copy(ys_ref.at[pl.ds(0, tm)], buf_ref.at[slot, k], sem.at[slot]).wait()
    acc = h_ref[...]
    cw = cw_ref[...]
    for k in range(TOP_K):
        acc = acc + cw[:, k:k + 1] * buf_ref[slot, k]
    h2_ref[...] = acc
    hn_ref[...] = _rms(acc, g_ref[...]).astype(hn_ref.dtype)


def moe_combine(h, cw, ys, pos_flat, norm_g, tm):
    t, d = h.shape
    grid_spec = pltpu.PrefetchScalarGridSpec(
        num_scalar_prefetch=1,
        grid=(t // tm,),
        in_specs=[pl.BlockSpec((tm, d), lambda i, pos: (i, 0)),
                  pl.BlockSpec((tm, LANE), lambda i, pos: (i, 0)),
                  pl.BlockSpec((1, d), lambda i, pos: (0, 0)),
                  pl.BlockSpec(memory_space=pl.ANY)],
        out_specs=[pl.BlockSpec((tm, d), lambda i, pos: (i, 0)),
                   pl.BlockSpec((tm, d), lambda i, pos: (i, 0))],
        scratch_shapes=[pltpu.VMEM((2, TOP_K, tm, d), F32), pltpu.SemaphoreType.DMA((2,))],
    )
    return pl.pallas_call(
        functools.partial(_combine_body, tm=tm),
        grid_spec=grid_spec,
        out_shape=[jax.ShapeDtypeStruct((t, d), F32), jax.ShapeDtypeStruct((t, d), BF16)],
        compiler_params=_cp(("arbitrary",)),
        name="moe_combine",
    )(pos_flat, h, cw, norm_g.reshape(1, -1).astype(F32), ys)


def _ple_body(a_ref, w_ref, p_ref, wp_ref, h_ref, o_ref):
    gate = _dot(a_ref[...], w_ref[...])
    pp = _dot(p_ref[...].astype(BF16), wp_ref[...])
    o_ref[...] = h_ref[...] + jax.nn.sigmoid(gate) * pp


def ple_gate(hn, w_gate, p, w_proj, h, *, tm=1024, tn=512):
    t, d = h.shape
    pd = p.shape[1]
    tm, tn = _tile(t, tm), _tile(d, tn)
    return pl.pallas_call(
        _ple_body,
        grid=(t // tm, d // tn),
        in_specs=[pl.BlockSpec((tm, d), lambda i, j: (i, 0)),
                  pl.BlockSpec((d, tn), lambda i, j: (0, j)),
                  pl.BlockSpec((tm, pd), lambda i, j: (i, 0)),
                  pl.BlockSpec((pd, tn), lambda i, j: (0, j)),
                  pl.BlockSpec((tm, tn), lambda i, j: (i, j))],
        out_specs=pl.BlockSpec((tm, tn), lambda i, j: (i, j)),
        out_shape=jax.ShapeDtypeStruct((t, d), F32),
        compiler_params=_cp(("parallel", "arbitrary")),
        name="ple_gate",
    )(hn, w_gate, p, w_proj, h)


def _split_w_in(w_in, qkw, vw, rank, qr, kvr, d_model):
    n_gla = 2 * qkw + 2 * vw
    off_gf = n_gla
    off_cq = off_gf + rank
    off_ckv = off_cq + qr
    off_kr = off_ckv + kvr
    off_gates = off_kr + MLA_ROPE
    kr = w_in[:, off_kr:off_gates]
    mid = jnp.concatenate([w_in[:, off_cq:off_kr], _pad_lane(kr), _pad_lane(_swap_halves(kr)),
                           _pad_lane(w_in[:, off_gf:off_cq])], axis=1)
    mid = jnp.pad(mid, ((0, 0), (0, -mid.shape[1] % 512)))
    lay = {"gq": 0, "gk": qkw, "gv": 2 * qkw, "gg": 2 * qkw + vw,
           "cq": 0, "ckv": qr, "krA": qr + kvr, "krB": qr + kvr + LANE, "gf": qr + kvr + 2 * LANE,
           "gate_gla": 0, "gate_mla": d_model}
    return (w_in[:, :n_gla].astype(BF16), mid.astype(BF16), w_in[:, off_gates:].astype(BF16)), lay


def _layer(h, p, w_in, gla_gate_up, gla_gate_bias, gla_out_norm, mla_q_norm, mla_kv_norm, mla_w_uq,
           mla_w_ukv, w_branch_gla, w_branch_mla, w_out, norm_mix, norm_moe, router_w, router_b,
           expert_w_gate_up, expert_b_gate_up, expert_w_down, expert_b_down, norm_ple, ple_gate_w,
           ple_proj_w, batch, seq):
    t, d = h.shape
    rank, qkw = gla_gate_up.shape
    vw = w_branch_gla.shape[0]
    qr, kvr = mla_q_norm.shape[0], mla_kv_norm.shape[0]
    (w_gla, w_mid, w_gates), lay = _split_w_in(w_in, qkw, vw, rank, qr, kvr, d)

    xn = rms_norm(h, norm_mix)
    proj_gla = matmul(xn, w_gla, out_dtype=BF16, name="in_proj_gla")
    proj_mid = matmul(xn, w_mid, out_dtype=BF16, name="in_proj_mid")
    proj_gates = matmul(xn, w_gates, out_dtype=BF16, name="in_proj_gates")
    o_gla = gla_mixer(proj_gla, proj_mid, lay, gla_gate_up, gla_gate_bias, gla_out_norm, batch, seq)
    o_mla = mla_mixer(proj_mid, lay, mla_q_norm, mla_kv_norm, mla_w_uq, mla_w_ukv, batch, seq)
    merged = merge_branches(o_gla, o_mla, w_branch_gla.astype(BF16), w_branch_mla.astype(BF16), proj_gates,
                            lay, d)
    h1 = matmul_residual(merged, w_out.astype(BF16), h, name="out_proj")

    ne = router_w.shape[1]
    tm_e = _tile(t, 256)
    hn, idx, rank_in, cw, cnt = moe_router(h1, norm_moe, router_w, router_b, tm_e)
    counts = cnt[0, :ne].astype(I32)
    tiles_e = -(-counts // tm_e)
    gend = jnp.cumsum(tiles_e) * tm_e
    gstart = gend - tiles_e * tm_e
    pos = (gstart[idx[:, :TOP_K]] + rank_in[:, :TOP_K]).reshape(-1)
    n_tiles = t * TOP_K // tm_e + ne
    tile_start = jnp.arange(n_tiles, dtype=I32) * tm_e
    tile_expert = jnp.minimum(jnp.sum((gend[None, :] <= tile_start[:, None]).astype(I32), axis=1), ne - 1)
    n_used = (gend[-1:] // tm_e).astype(I32)
    sched = _weight_stream_schedule(tile_expert, n_used, tiles_e, gstart // tm_e)
    src = moe_sources(pos, n_tiles * tm_e)
    ys = moe_experts(hn, src, tile_expert, n_used, sched, expert_w_gate_up, expert_b_gate_up,
                     expert_w_down, expert_b_down, tm_e)
    h2, hn2 = moe_combine(h1, cw, ys, pos, norm_ple, _tile(t, 128))

    return ple_gate(hn2, ple_gate_w.astype(BF16), p, ple_proj_w.astype(BF16), h2)


def kernel(x, p, w_in, gla_gate_up, gla_gate_bias, gla_out_norm, mla_q_norm, mla_kv_norm, mla_w_uq, mla_w_ukv, w_branch_gla, w_branch_mla, w_out, norm_mix, norm_moe, router_w, router_b, expert_w_gate_up, expert_b_gate_up, expert_w_down, expert_b_down, norm_ple, ple_gate_w, ple_proj_w, final_norm):
    batch, seq, d = x.shape
    h = x.reshape(batch * seq, d)
    for i in range(p.shape[0]):
        h = _layer(h, p[i].reshape(batch * seq, -1), w_in[i], gla_gate_up[i], gla_gate_bias[i],
                   gla_out_norm[i], mla_q_norm[i], mla_kv_norm[i], mla_w_uq[i], mla_w_ukv[i],
                   w_branch_gla[i], w_branch_mla[i], w_out[i], norm_mix[i], norm_moe[i], router_w[i],
                   router_b[i], expert_w_gate_up[i], expert_b_gate_up[i], expert_w_down[i],
                   expert_b_down[i], norm_ple[i], ple_gate_w[i], ple_proj_w[i], batch, seq)
    return rms_norm(h, final_norm, out_dtype=F32).reshape(batch, seq, d)
```

```python
import functools
import math

import jax
import jax.numpy as jnp
from jax import lax
from jax.experimental import pallas as pl
from jax.experimental.pallas import tpu as pltpu

F32 = jnp.float32
BF16 = jnp.bfloat16
U32 = jnp.uint32
I32 = jnp.int32

CHUNK = 64
EPS = 1e-6
GLA_HEADS = 8
GLA_TAU = 16.0
MLA_HEADS = 16
MLA_NOPE = 128
MLA_ROPE = 64
MLA_V = 128
ROPE_THETA = 10000.0
TOP_K = 4
SWIGLU_LIMIT = 7.0
SWIGLU_ALPHA = 1.702

LANE = 128
SUBLANE = 8
VMEM_LIMIT = 56 * 2 ** 20
NEG_BIG = -1e30


def _tile(n, pref):
    if n <= pref:
        return n
    t = pref
    while t >= 8:
        if n % t == 0 and t % 8 == 0:
            return t
        t -= 8
    return n


def _cp(sem):
    return pltpu.CompilerParams(dimension_semantics=sem, vmem_limit_bytes=VMEM_LIMIT)


def _dot(a, b):
    return jnp.dot(a, b, preferred_element_type=F32)


def _dot_nt(a, b):
    return lax.dot_general(a, b, (((1,), (1,)), ((), ())), preferred_element_type=F32)


def _dot_tn(a, b):
    return lax.dot_general(a, b, (((0,), (0,)), ((), ())), preferred_element_type=F32)


def _rms(x, g):
    return x * lax.rsqrt(jnp.mean(x * x, axis=-1, keepdims=True) + EPS) * g


def _rms_body(x_ref, g_ref, o_ref):
    o_ref[...] = _rms(x_ref[...].astype(F32), g_ref[...]).astype(o_ref.dtype)


def rms_norm(x, g, *, col_block=0, width=None, out_dtype=BF16, tm=512):
    m = x.shape[0]
    width = width or x.shape[1]
    tm = _tile(m, tm)
    return pl.pallas_call(
        _rms_body,
        grid=(m // tm,),
        in_specs=[pl.BlockSpec((tm, width), lambda i: (i, col_block)),
                  pl.BlockSpec((1, width), lambda i: (0, 0))],
        out_specs=pl.BlockSpec((tm, width), lambda i: (i, 0)),
        out_shape=jax.ShapeDtypeStruct((m, width), out_dtype),
        compiler_params=_cp(("parallel",)),
        name="rms_norm",
    )(x, g.reshape(1, -1).astype(F32))


def _mm_body(a_ref, w_ref, o_ref):
    o_ref[...] = _dot(a_ref[...], w_ref[...]).astype(o_ref.dtype)


def matmul(a, w, *, out_dtype, tm=1024, tn=1024, name="matmul"):
    m, k = a.shape
    n = w.shape[1]
    tm, tn = _tile(m, tm), _tile(n, tn)
    return pl.pallas_call(
        _mm_body,
        grid=(m // tm, n // tn),
        in_specs=[pl.BlockSpec((tm, k), lambda i, j: (i, 0)),
                  pl.BlockSpec((k, tn), lambda i, j: (0, j))],
        out_specs=pl.BlockSpec((tm, tn), lambda i, j: (i, j)),
        out_shape=jax.ShapeDtypeStruct((m, n), out_dtype),
        compiler_params=_cp(("parallel", "arbitrary")),
        name=name,
    )(a, w)


def _mm_wcast_body(a_ref, w_ref, *rest):
    (res_ref,), (o_ref, wb_ref) = (rest[:-2] or (None,)), rest[-2:]

    @pl.when(pl.program_id(1) == 0)
    def _():
        wb_ref[...] = w_ref[...].astype(BF16)

    acc = _dot(a_ref[...], wb_ref[...])
    if res_ref is not None:
        acc = res_ref[...] + acc
    o_ref[...] = acc.astype(o_ref.dtype)


def matmul_f32_weight(a, w, *, n_cols=None, res=None, out_dtype, tm=1024, tn=512, name):
    m, k = a.shape
    n = n_cols or w.shape[1]
    tm, tn = _tile(m, tm), _tile(n, tn)
    in_specs = [pl.BlockSpec((tm, k), lambda j, i: (i, 0)),
                pl.BlockSpec((k, tn), lambda j, i: (0, j))]
    args = [a, w]
    if res is not None:
        in_specs.append(pl.BlockSpec((tm, tn), lambda j, i: (i, j)))
        args.append(res)
    return pl.pallas_call(
        _mm_wcast_body,
        grid=(n // tn, m // tm),
        in_specs=in_specs,
        out_specs=pl.BlockSpec((tm, tn), lambda j, i: (i, j)),
        out_shape=jax.ShapeDtypeStruct((m, n), out_dtype),
        scratch_shapes=[pltpu.VMEM((k, tn), BF16)],
        compiler_params=_cp(("parallel", "arbitrary")),
        name=name,
    )(*args)


def _gla_body(q_ref, k_ref, v_ref, gg_ref, gf_ref, gup_ref, gb_ref, on_ref, tri_ref, blk_ref,
              o_ref, *scratch, nchunk, scale, hg, dk, dv):
    st_refs, ob_refs = scratch[:hg], scratch[hg:]

    @pl.when(pl.program_id(2) == 0)
    def _():
        for st_ref in st_refs:
            st_ref[...] = jnp.zeros_like(st_ref)

    logit = _dot(gf_ref[...], gup_ref[...]) + gb_ref[...]
    la = (jnp.minimum(logit, 0.0) - jnp.log(1.0 + jnp.exp(-jnp.abs(logit)))) * (1.0 / GLA_TAU)
    la_hi = la.astype(BF16)
    la_lo = (la - la_hi.astype(F32)).astype(BF16)
    tri = tri_ref[...]
    blk = blk_ref[...]
    b = _dot(tri, la_hi) + _dot(tri, la_lo)
    bl = _dot(blk, la_hi) + _dot(blk, la_lo)
    q = q_ref[...].astype(F32)
    k = k_ref[...].astype(F32)
    qd = (q * scale * jnp.exp(b)).astype(BF16)
    kd = (k * jnp.exp(-b)).astype(BF16)
    ke = (k * jnp.exp(bl - b)).astype(BF16)
    dec = jnp.exp(bl)
    row = lax.broadcasted_iota(I32, (CHUNK, CHUNK), 0)
    col = lax.broadcasted_iota(I32, (CHUNK, CHUNK), 1)
    causal = row >= col
    for c in range(nchunk):
        lo = c * CHUNK
        for h in range(hg):
            ks, vs = slice(h * dk, (h + 1) * dk), slice(h * dv, (h + 1) * dv)
            qc = qd[lo:lo + CHUNK, ks]
            kc = kd[lo:lo + CHUNK, ks]
            kec = ke[lo:lo + CHUNK, ks]
            vc = v_ref[lo:lo + CHUNK, vs]
            s = jnp.where(causal, _dot_nt(qc, kc), 0.0).astype(BF16)
            st = st_refs[h][...]
            ob_refs[h][lo:lo + CHUNK, :] = _dot(s, vc) + _dot_nt(qc, st.astype(BF16))
            st_refs[h][...] = st * dec[lo:lo + 1, ks] + _dot_tn(vc, kec)
    for h in range(hg):
        vs = slice(h * dv, (h + 1) * dv)
        o = _rms(ob_refs[h][...], on_ref[...])
        g = gg_ref[:, vs].astype(F32)
        o_ref[:, vs] = (o * (g * jax.nn.sigmoid(g))).astype(o_ref.dtype)


def gla_mixer(proj, proj_mid, lay, gate_up, gate_bias, out_norm, batch, seq):
    heads = GLA_HEADS
    dk = gate_up.shape[1] // heads
    dv = out_norm.shape[0]
    rank = gate_up.shape[0]
    lb = _tile(seq, 512)
    nchunk = lb // CHUNK
    nsb = seq // lb
    t = batch * seq
    gup = jnp.zeros((LANE, heads * dk), F32).at[:rank].set(gate_up).astype(BF16)
    r = jnp.arange(lb)
    same = (r[:, None] // CHUNK) == (r[None, :] // CHUNK)
    tri = (same & (r[None, :] <= r[:, None])).astype(BF16)
    blk = same.astype(BF16)
    hg = 2 if heads % 2 == 0 else 1
    wk, wv = hg * dk, hg * dv
    qb, kb = lay["gq"] // wk, lay["gk"] // wk
    vb, gb = lay["gv"] // wv, lay["gg"] // wv
    fb = lay["gf"] // LANE
    rowmap = lambda b, h, s: b * nsb + s
    return pl.pallas_call(
        functools.partial(_gla_body, nchunk=nchunk, scale=dk ** -0.5, hg=hg, dk=dk, dv=dv),
        grid=(batch, heads // hg, nsb),
        in_specs=[
            pl.BlockSpec((lb, wk), lambda b, h, s: (rowmap(b, h, s), qb + h)),
            pl.BlockSpec((lb, wk), lambda b, h, s: (rowmap(b, h, s), kb + h)),
            pl.BlockSpec((lb, wv), lambda b, h, s: (rowmap(b, h, s), vb + h)),
            pl.BlockSpec((lb, wv), lambda b, h, s: (rowmap(b, h, s), gb + h)),
            pl.BlockSpec((lb, LANE), lambda b, h, s: (rowmap(b, h, s), fb)),
            pl.BlockSpec((LANE, wk), lambda b, h, s: (0, h)),
            pl.BlockSpec((1, wk), lambda b, h, s: (0, h)),
            pl.BlockSpec((1, dv), lambda b, h, s: (0, 0)),
            pl.BlockSpec((lb, lb), lambda b, h, s: (0, 0)),
            pl.BlockSpec((lb, lb), lambda b, h, s: (0, 0)),
        ],
        out_specs=pl.BlockSpec((lb, wv), lambda b, h, s: (rowmap(b, h, s), h)),
        out_shape=jax.ShapeDtypeStruct((t, heads * dv), BF16),
        scratch_shapes=[pltpu.VMEM((dv, dk), F32)] * hg + [pltpu.VMEM((lb, dv), F32)] * hg,
        compiler_params=_cp(("parallel", "parallel", "arbitrary")),
        name="gla_mixer",
    )(proj, proj, proj, proj, proj_mid, gup, gate_bias.reshape(1, -1).astype(F32),
      out_norm.reshape(1, -1).astype(F32), tri, blk)


def _q_up_body(a_ref, w_ref, c_ref, s_ref, o_ref, *, hpt, scale):
    acc = _dot(a_ref[...], w_ref[...])
    c = c_ref[...]
    s = s_ref[...]
    for h in range(hpt):
        src = h * 3 * LANE
        dst = h * 2 * LANE
        o_ref[:, dst:dst + LANE] = (acc[:, src:src + LANE] * scale).astype(o_ref.dtype)
        pe = acc[:, src + LANE:src + 2 * LANE] * c + acc[:, src + 2 * LANE:src + 3 * LANE] * s
        o_ref[:, dst + LANE:dst + 2 * LANE] = (pe * scale).astype(o_ref.dtype)


def _k_up_body(a_ref, w_ref, kra_ref, krb_ref, c_ref, s_ref, o_ref, *, hpt):
    acc = _dot(a_ref[...], w_ref[...])
    kpe = (kra_ref[...].astype(F32) * c_ref[...] + krb_ref[...].astype(F32) * s_ref[...]).astype(o_ref.dtype)
    for h in range(hpt):
        o_ref[:, 2 * h * LANE:(2 * h + 1) * LANE] = acc[:, h * LANE:(h + 1) * LANE].astype(o_ref.dtype)
        o_ref[:, (2 * h + 1) * LANE:(2 * h + 2) * LANE] = kpe


def _v_up_body(a_ref, w_ref, o_ref, *, hpt):
    acc = _dot(a_ref[...], w_ref[...])
    lane = lax.broadcasted_iota(I32, (acc.shape[0], LANE), 1)
    ones_col = (lane == 0).astype(o_ref.dtype)
    for h in range(hpt):
        o_ref[:, 2 * h * LANE:(2 * h + 1) * LANE] = acc[:, h * LANE:(h + 1) * LANE].astype(o_ref.dtype)
        o_ref[:, (2 * h + 1) * LANE:(2 * h + 2) * LANE] = ones_col


def _swap_halves(w):
    half = w.shape[-1] // 2
    return jnp.concatenate([w[..., half:], w[..., :half]], axis=-1)


def _pad_lane(w):
    pad = LANE - w.shape[-1]
    return jnp.pad(w, [(0, 0)] * (w.ndim - 1) + [(0, pad)])


def mla_mixer(proj, lay, q_norm, kv_norm, w_uq, w_ukv, batch, seq):
    heads = MLA_HEADS
    t = batch * seq
    qr, kvr = q_norm.shape[0], kv_norm.shape[0]
    assert MLA_NOPE == LANE and MLA_V == LANE and MLA_ROPE <= LANE
    scale = (MLA_NOPE + MLA_ROPE) ** -0.5 * math.log2(math.e)
    tm = _tile(seq, 1024)
    nrb = seq // tm
    hpt = 4 if heads % 4 == 0 else 1

    pos = jnp.arange(seq, dtype=F32)
    inv_freq = ROPE_THETA ** (-jnp.arange(0, MLA_ROPE, 2, dtype=F32) / MLA_ROPE)
    ang = pos[:, None] * inv_freq[None, :]
    cos, sin = jnp.cos(ang), jnp.sin(ang)
    ctab = _pad_lane(jnp.concatenate([cos, cos], axis=-1))
    stab = _pad_lane(jnp.concatenate([-sin, sin], axis=-1))

    cqn = rms_norm(proj, q_norm, col_block=lay["cq"] // qr, width=qr)
    ckvn = rms_norm(proj, kv_norm, col_block=lay["ckv"] // kvr, width=kvr)

    wq = w_uq.reshape(qr, heads, MLA_NOPE + MLA_ROPE)
    wq_pe = wq[..., MLA_NOPE:]
    wq_r = jnp.concatenate([wq[..., :MLA_NOPE], _pad_lane(wq_pe), _pad_lane(_swap_halves(wq_pe))],
                           axis=-1).reshape(qr, heads * 3 * LANE).astype(BF16)
    q_cat = pl.pallas_call(
        functools.partial(_q_up_body, hpt=hpt, scale=scale),
        grid=(t // tm, heads // hpt),
        in_specs=[pl.BlockSpec((tm, qr), lambda i, j: (i, 0)),
                  pl.BlockSpec((qr, hpt * 3 * LANE), lambda i, j: (0, j)),
                  pl.BlockSpec((tm, LANE), lambda i, j: (i % nrb, 0)),
                  pl.BlockSpec((tm, LANE), lambda i, j: (i % nrb, 0))],
        out_specs=pl.BlockSpec((tm, hpt * 2 * LANE), lambda i, j: (i, j)),
        out_shape=jax.ShapeDtypeStruct((t, heads * 2 * LANE), BF16),
        compiler_params=_cp(("parallel", "arbitrary")),
        name="mla_q_up",
    )(cqn, wq_r, ctab, stab)

    wkv = w_ukv.reshape(kvr, heads, MLA_NOPE + MLA_V)
    wk = wkv[..., :MLA_NOPE].reshape(kvr, heads * MLA_NOPE).astype(BF16)
    wv = wkv[..., MLA_NOPE:].reshape(kvr, heads * MLA_V).astype(BF16)
    ka, kb = lay["krA"] // LANE, lay["krB"] // LANE
    k_cat = pl.pallas_call(
        functools.partial(_k_up_body, hpt=hpt),
        grid=(t // tm, heads // hpt),
        in_specs=[pl.BlockSpec((tm, kvr), lambda i, j: (i, 0)),
                  pl.BlockSpec((kvr, hpt * LANE), lambda i, j: (0, j)),
                  pl.BlockSpec((tm, LANE), lambda i, j: (i, ka)),
                  pl.BlockSpec((tm, LANE), lambda i, j: (i, kb)),
                  pl.BlockSpec((tm, LANE), lambda i, j: (i % nrb, 0)),
                  pl.BlockSpec((tm, LANE), lambda i, j: (i % nrb, 0))],
        out_specs=pl.BlockSpec((tm, hpt * 2 * LANE), lambda i, j: (i, j)),
        out_shape=jax.ShapeDtypeStruct((t, heads * 2 * LANE), BF16),
        compiler_params=_cp(("parallel", "arbitrary")),
        name="mla_k_up",
    )(ckvn, wk, proj, proj, ctab, stab)
    v_ext = pl.pallas_call(
        functools.partial(_v_up_body, hpt=hpt),
        grid=(t // tm, heads // hpt),
        in_specs=[pl.BlockSpec((tm, kvr), lambda i, j: (i, 0)),
                  pl.BlockSpec((kvr, hpt * LANE), lambda i, j: (0, j))],
        out_specs=pl.BlockSpec((tm, hpt * 2 * LANE), lambda i, j: (i, j)),
        out_shape=jax.ShapeDtypeStruct((t, heads * 2 * LANE), BF16),
        compiler_params=_cp(("parallel", "arbitrary")),
        name="mla_v_up",
    )(ckvn, wv)
    return mla_attention(q_cat, k_cat, v_ext, batch, seq)


def _mla_attn_body(q_ref, k_ref, v_ref, o_ref, m_ref, acc_ref, *, tq, tk, hg):
    qi = pl.program_id(2)
    m_ref[...] = jnp.full_like(m_ref, NEG_BIG)
    acc_ref[...] = jnp.zeros_like(acc_ref)
    w = 2 * LANE

    def block(ki, masked):
        ks = pl.multiple_of(ki * tk, tk)
        for g in range(hg):
            s = _dot_nt(q_ref[:, g * w:(g + 1) * w], k_ref[pl.ds(ks, tk), g * w:(g + 1) * w])
            if masked:
                qc = (qi * tq + lax.broadcasted_iota(I32, (tq, tk), 0)) // CHUNK
                kc = (ki * tk + lax.broadcasted_iota(I32, (tq, tk), 1)) // CHUNK
                s = jnp.where(kc <= qc, s, NEG_BIG)
            m_prev = m_ref[g]
            m_new = jnp.maximum(m_prev, jnp.max(s, axis=-1, keepdims=True))
            p = jnp.exp2(s - m_new).astype(BF16)
            acc_ref[g] = (jnp.exp2(m_prev - m_new) * acc_ref[g]
                          + _dot(p, v_ref[pl.ds(ks, tk), g * w:(g + 1) * w]))
            m_ref[g] = m_new

    def body(ki, carry):
        block(ki, False)
        return carry

    n_full = qi * (tq // tk)
    lax.fori_loop(0, n_full, body, 0)
    for dblk in range(tq // tk):
        block(n_full + dblk, True)
    for g in range(hg):
        acc = acc_ref[g]
        o_ref[:, g * LANE:(g + 1) * LANE] = (acc[:, :LANE] / acc[:, LANE:LANE + 1]).astype(o_ref.dtype)


def mla_attention(q_cat, k_cat, v_ext, batch, seq):
    heads = MLA_HEADS
    t = batch * seq
    tq = _tile(seq, 512)
    tk = tq
    nq = seq // tq
    hg = 4 if heads % 4 == 0 else 1
    w = 2 * LANE
    return pl.pallas_call(
        functools.partial(_mla_attn_body, tq=tq, tk=tk, hg=hg),
        grid=(batch, heads // hg, nq),
        in_specs=[pl.BlockSpec((tq, hg * w), lambda b, h, i: (b * nq + i, h)),
                  pl.BlockSpec((seq, hg * w), lambda b, h, i: (b, h)),
                  pl.BlockSpec((seq, hg * w), lambda b, h, i: (b, h))],
        out_specs=pl.BlockSpec((tq, hg * LANE), lambda b, h, i: (b * nq + i, h)),
        out_shape=jax.ShapeDtypeStruct((t, heads * MLA_V), BF16),
        scratch_shapes=[pltpu.VMEM((hg, tq, 1), F32), pltpu.VMEM((hg, tq, w), F32)],
        compiler_params=_cp(("parallel", "parallel", "arbitrary")),
        name="mla_attention",
    )(q_cat, k_cat, v_ext)


def _merge_body(a1_ref, w1_ref, a2_ref, w2_ref, g1_ref, g2_ref, o_ref, wb1_ref, wb2_ref):
    @pl.when(pl.program_id(1) == 0)
    def _():
        wb1_ref[...] = w1_ref[...].astype(BF16)
        wb2_ref[...] = w2_ref[...].astype(BF16)

    y1 = _dot(a1_ref[...], wb1_ref[...])
    y2 = _dot(a2_ref[...], wb2_ref[...])
    o_ref[...] = (jax.nn.sigmoid(g1_ref[...].astype(F32)) * y1
                  + jax.nn.sigmoid(g2_ref[...].astype(F32)) * y2).astype(o_ref.dtype)


def merge_branches(o_gla, o_mla, w_gla, w_mla, proj, lay, d_model):
    t = o_gla.shape[0]
    tm = _tile(t, 1024)
    tn = _tile(d_model, 512)
    g1, g2 = lay["gate_gla"] // tn, lay["gate_mla"] // tn
    k1, k2 = o_gla.shape[1], o_mla.shape[1]
    return pl.pallas_call(
        _merge_body,
        grid=(d_model // tn, t // tm),
        in_specs=[pl.BlockSpec((tm, k1), lambda j, i: (i, 0)),
                  pl.BlockSpec((k1, tn), lambda j, i: (0, j)),
                  pl.BlockSpec((tm, k2), lambda j, i: (i, 0)),
                  pl.BlockSpec((k2, tn), lambda j, i: (0, j)),
                  pl.BlockSpec((tm, tn), lambda j, i: (i, g1 + j)),
                  pl.BlockSpec((tm, tn), lambda j, i: (i, g2 + j))],
        out_specs=pl.BlockSpec((tm, tn), lambda j, i: (i, j)),
        out_shape=jax.ShapeDtypeStruct((t, d_model), BF16),
        scratch_shapes=[pltpu.VMEM((k1, tn), BF16), pltpu.VMEM((k2, tn), BF16)],
        compiler_params=_cp(("parallel", "arbitrary")),
        name="merge_branches",
    )(o_gla, w_gla, o_mla, w_mla, proj, proj)


def _router_body(h_ref, g_ref, rwh_ref, rwl_ref, rb_ref, tri_ref, hn_ref, idx_ref, rank_ref, cw_ref, cnt_ref,
                 carry_ref):
    @pl.when(pl.program_id(0) == 0)
    def _():
        carry_ref[...] = jnp.zeros_like(carry_ref)

    hn = _rms(h_ref[...], g_ref[...])
    hn_ref[...] = hn
    hn_hi = hn.astype(BF16)
    hn_lo = (hn - hn_hi.astype(F32)).astype(BF16)
    logits = (_dot(hn_hi, rwh_ref[...]) + _dot(hn_hi, rwl_ref[...]) + _dot(hn_lo, rwh_ref[...])) + rb_ref[...]
    tm = logits.shape[0]
    lane = lax.broadcasted_iota(I32, (tm, LANE), 1)
    work = logits
    vals, sels, idxs = [], [], []
    for _ in range(TOP_K):
        mx = jnp.max(work, axis=-1, keepdims=True)
        ix = jnp.min(jnp.where(work == mx, lane, LANE), axis=-1, keepdims=True)
        sel = lane == ix
        work = jnp.where(sel, -jnp.inf, work)
        vals.append(mx)
        sels.append(sel)
        idxs.append(ix)
    es = [jnp.exp(v - vals[0]) for v in vals]
    den = es[0]
    for e in es[1:]:
        den = den + e
    hot = jnp.zeros((tm, LANE), F32)
    for sel in sels:
        hot = hot + sel.astype(F32)
    carry = carry_ref[0:1, :]
    rank = _dot(tri_ref[...], hot.astype(BF16)) + carry
    idx_out = jnp.zeros((tm, LANE), I32)
    rank_out = jnp.zeros((tm, LANE), F32)
    cw_out = jnp.zeros((tm, LANE), F32)
    for k in range(TOP_K):
        rk = jnp.sum(jnp.where(sels[k], rank, 0.0), axis=-1, keepdims=True)
        idx_out = jnp.where(lane == k, idxs[k], idx_out)
        rank_out = jnp.where(lane == k, rk, rank_out)
        cw_out = jnp.where(lane == k, es[k] / den, cw_out)
    idx_ref[...] = idx_out
    rank_ref[...] = rank_out.astype(I32)
    cw_ref[...] = cw_out
    new_carry = carry + jnp.sum(hot, axis=0, keepdims=True)
    carry_ref[...] = jnp.broadcast_to(new_carry, carry_ref.shape)
    cnt_ref[...] = jnp.broadcast_to(new_carry, cnt_ref.shape)


def moe_router(h, norm_g, router_w, router_b, tm):
    t, d = h.shape
    ne = router_w.shape[1]
    assert ne <= LANE and d % (2 * LANE) == 0
    rw = jnp.zeros((d, LANE), F32).at[:, :ne].set(router_w)
    rw_hi = rw.astype(BF16)
    rw_lo = (rw - rw_hi.astype(F32)).astype(BF16)
    rb = jnp.full((1, LANE), NEG_BIG, F32).at[0, :ne].set(router_b)
    r = jnp.arange(tm)
    tri = (r[None, :] < r[:, None]).astype(BF16)
    row = lambda i: (i, 0)
    const = lambda i: (0, 0)
    return pl.pallas_call(
        _router_body,
        grid=(t // tm,),
        in_specs=[pl.BlockSpec((tm, d), row), pl.BlockSpec((1, d), const),
                  pl.BlockSpec((d, LANE), const), pl.BlockSpec((d, LANE), const),
                  pl.BlockSpec((1, LANE), const), pl.BlockSpec((tm, tm), const)],
        out_specs=[pl.BlockSpec((tm, d), row), pl.BlockSpec((tm, LANE), row),
                   pl.BlockSpec((tm, LANE), row), pl.BlockSpec((tm, LANE), row),
                   pl.BlockSpec((8, LANE), const)],
        out_shape=[jax.ShapeDtypeStruct((t, d), F32), jax.ShapeDtypeStruct((t, LANE), I32),
                   jax.ShapeDtypeStruct((t, LANE), I32), jax.ShapeDtypeStruct((t, LANE), F32),
                   jax.ShapeDtypeStruct((8, LANE), F32)],
        scratch_shapes=[pltpu.VMEM((8, LANE), F32)],
        compiler_params=_cp(("arbitrary",)),
        name="moe_router",
    )(h, norm_g.reshape(1, -1).astype(F32), rw_hi, rw_lo, rb, tri)


def _sources_body(pos_ref, src_ref, *, nsteps):
    i = pl.program_id(0)
    rows_per = src_ref.shape[0] // nsteps
    toks_per = pos_ref.shape[0] // TOP_K // nsteps

    @pl.when(i < nsteps)
    def _():
        def clear(r, carry):
            src_ref[i * rows_per + r] = 0
            return carry

        lax.fori_loop(0, rows_per, clear, 0, unroll=8)

    @pl.when(i >= nsteps)
    def _():
        t0 = (i - nsteps) * toks_per

        def put(j, carry):
            for k in range(TOP_K):
                src_ref[pos_ref[(t0 + j) * TOP_K + k]] = t0 + j
            return carry

        lax.fori_loop(0, toks_per, put, 0, unroll=2)


def moe_sources(pos_flat, n_rows):
    n_tok = pos_flat.shape[0] // TOP_K
    nsteps = 64 if (n_rows % 64 == 0 and n_tok % 64 == 0) else 1
    return pl.pallas_call(
        functools.partial(_sources_body, nsteps=nsteps),
        grid=(2 * nsteps,),
        in_specs=[pl.BlockSpec(memory_space=pltpu.SMEM)],
        out_specs=pl.BlockSpec(memory_space=pltpu.SMEM),
        out_shape=jax.ShapeDtypeStruct((n_rows,), I32),
        compiler_params=_cp(("arbitrary",)),
        name="moe_sources",
    )(pos_flat)


W_CHUNKS = 16


def _moe_up_body(te_ref, nu_ref, clo_ref, chi_ref, nxt_ref, slot_ref, src_ref, hn_ref, w_ref, b_ref, o_ref,
                 xf_ref, xb_ref, wb_ref, st_ref, gsem, wsem, *, tm, de):
    i = pl.program_id(0)
    ch = w_ref.shape[1] // W_CHUNKS

    def w_copy(e, c, buf):
        return pltpu.make_async_copy(w_ref.at[e, pl.ds(c * ch, ch)], st_ref.at[buf], wsem.at[buf])

    def convert(c, buf, slot):
        wb_ref[slot, pl.ds(pl.multiple_of(c * ch, ch), ch), :] = st_ref[buf].astype(BF16)

    def gather(tile):
        base = tile * tm

        def rows8(g, carry):
            r0 = pl.multiple_of(g * SUBLANE, SUBLANE)
            for j in range(SUBLANE):
                pltpu.make_async_copy(hn_ref.at[pl.ds(src_ref[base + r0 + j], 1)],
                                      xf_ref.at[pl.ds(r0 + j, 1)], gsem).start()
            return carry

        lax.fori_loop(0, tm // SUBLANE, rows8, 0)

    @pl.when(i == 0)
    def _():
        e0, s0 = te_ref[0], slot_ref[0]
        w_copy(e0, 0, 0).start()

        def load(c, carry):
            @pl.when(c + 1 < W_CHUNKS)
            def _():
                w_copy(e0, c + 1, (c + 1) % 2).start()

            w_copy(e0, c, c % 2).wait()
            convert(c, c % 2, s0)
            return carry

        lax.fori_loop(0, W_CHUNKS, load, 0)
        gather(0)

    @pl.when(i < nu_ref[0])
    def _():
        pltpu.make_async_copy(hn_ref.at[pl.ds(0, tm)], xf_ref, gsem).wait()
        xb_ref[...] = xf_ref[...].astype(BF16)

        @pl.when(i + 1 < nu_ref[0])
        def _():
            gather(i + 1)

        e_n, s_n = nxt_ref[i], 1 - slot_ref[i]
        lo, hi = clo_ref[i], chi_ref[i]

        @pl.when((lo == 0) & (hi > 0))
        def _():
            w_copy(e_n, 0, 0).start()
            w_copy(e_n, 1, 1).start()

        def chunk(c, carry):
            w_copy(e_n, c, c % 2).wait()
            convert(c, c % 2, s_n)

            @pl.when(c + 2 < W_CHUNKS)
            def _():
                w_copy(e_n, c + 2, c % 2).start()

            return carry

        lax.fori_loop(lo, hi, chunk, 0)

        gu = _dot(xb_ref[...], wb_ref[slot_ref[i]]) + b_ref[...]
        gate = jnp.minimum(gu[:, :de], SWIGLU_LIMIT)
        up = jnp.clip(gu[:, de:], -SWIGLU_LIMIT, SWIGLU_LIMIT)
        o_ref[...] = ((up + 1.0) * (gate * jax.nn.sigmoid(SWIGLU_ALPHA * gate))).astype(o_ref.dtype)

    @pl.when(i >= nu_ref[0])
    def _():
        o_ref[...] = jnp.zeros_like(o_ref)


def _moe_down_body(te_ref, nu_ref, a_ref, w_ref, b_ref, o_ref, wb_ref):
    i = pl.program_id(0)

    @pl.when(i < nu_ref[0])
    def _():
        @pl.when((i == 0) | (te_ref[i] != te_ref[jnp.maximum(i - 1, 0)]))
        def _():
            wb_ref[...] = w_ref[...].astype(BF16)

        o_ref[...] = _dot(a_ref[...], wb_ref[...]) + b_ref[...]

    @pl.when(i >= nu_ref[0])
    def _():
        o_ref[...] = jnp.zeros_like(o_ref)


def _weight_stream_schedule(tile_expert, n_used, tiles_per_expert, group_start_tile):
    ne = tiles_per_expert.shape[0]
    n_tiles = tile_expert.shape[0]
    nonempty = tiles_per_expert > 0
    ordinal = jnp.cumsum(nonempty.astype(I32)) - 1
    cand = jnp.where(nonempty, jnp.arange(ne, dtype=I32), ne)
    next_ge = lax.cummin(cand, axis=0, reverse=True)
    next_gt = jnp.concatenate([next_ge[1:], jnp.full((1,), ne, I32)])
    tile = jnp.arange(n_tiles, dtype=I32)
    k_in = tile - group_start_tile[tile_expert]
    n_in = jnp.maximum(tiles_per_expert[tile_expert], 1)
    nxt = next_gt[tile_expert]
    live = (nxt < ne) & (tile < n_used[0])
    c_lo = jnp.where(live, k_in * W_CHUNKS // n_in, 0).astype(I32)
    c_hi = jnp.where(live, (k_in + 1) * W_CHUNKS // n_in, 0).astype(I32)
    return c_lo, c_hi, jnp.where(live, nxt, 0).astype(I32), (ordinal[tile_expert] % 2).astype(I32)


def moe_experts(hn, src, tile_expert, n_used, sched, w_gu, b_gu, w_dn, b_dn, tm):
    d = hn.shape[1]
    n_rows = src.shape[0]
    ne, _, de2 = w_gu.shape
    de = de2 // 2
    n_tiles = n_rows // tm
    assert d % W_CHUNKS == 0 and W_CHUNKS >= 2
    c_lo, c_hi, nxt, slot = sched
    act = pl.pallas_call(
        functools.partial(_moe_up_body, tm=tm, de=de),
        grid_spec=pltpu.PrefetchScalarGridSpec(
            num_scalar_prefetch=7,
            grid=(n_tiles,),
            in_specs=[pl.BlockSpec(memory_space=pl.ANY),
                      pl.BlockSpec(memory_space=pl.ANY),
                      pl.BlockSpec((None, 1, de2), lambda i, te, *_: (te[i], 0, 0))],
            out_specs=pl.BlockSpec((tm, de), lambda i, *_: (i, 0)),
            scratch_shapes=[pltpu.VMEM((tm, d), F32), pltpu.VMEM((tm, d), BF16),
                            pltpu.VMEM((2, d, de2), BF16), pltpu.VMEM((2, d // W_CHUNKS, de2), F32),
                            pltpu.SemaphoreType.DMA(()), pltpu.SemaphoreType.DMA((2,))],
        ),
        out_shape=jax.ShapeDtypeStruct((n_rows, de), BF16),
        compiler_params=_cp(("arbitrary",)),
        name="moe_up",
    )(tile_expert, n_used, c_lo, c_hi, nxt, slot, src, hn, w_gu, b_gu.reshape(ne, 1, de2))
    return pl.pallas_call(
        _moe_down_body,
        grid_spec=pltpu.PrefetchScalarGridSpec(
            num_scalar_prefetch=2,
            grid=(n_tiles,),
            in_specs=[pl.BlockSpec((tm, de), lambda i, te, nu: (i, 0)),
                      pl.BlockSpec((None, de, d), lambda i, te, nu: (te[i], 0, 0)),
                      pl.BlockSpec((None, 1, d), lambda i, te, nu: (te[i], 0, 0))],
            out_specs=pl.BlockSpec((tm, d), lambda i, te, nu: (i, 0)),
            scratch_shapes=[pltpu.VMEM((de, d), BF16)],
        ),
        out_shape=jax.ShapeDtypeStruct((n_rows, d), F32),
        compiler_params=_cp(("arbitrary",)),
        name="moe_down",
    )(tile_expert, n_used, act, w_dn, b_dn.reshape(ne, 1, d))


def _combine_body(pos_ref, h_ref, cw_ref, g_ref, ys_ref, h2_ref, hn_ref, buf_ref, sem, *, tm):
    i = pl.program_id(0)

    def issue(tile, slot):
        base = tile * (tm * TOP_K)

        def row(r, carry):
            for k in range(TOP_K):
                p = pos_ref[base + r * TOP_K + k]
                pltpu.make_async_copy(ys_ref.at[pl.ds(p, 1)], buf_ref.at[slot, k, pl.ds(r, 1)],
                                      sem.at[slot]).start()
            return carry

        lax.fori_loop(0, tm, row, 0, unroll=2)

    @pl.when(i == 0)
    def _():
        issue(0, 0)

    @pl.when(i + 1 < pl.num_programs(0))
    def _():
        issue(i + 1, (i + 1) % 2)

    slot = i % 2
    for k in range(TOP_K):
        pltpu.make_async_copy(ys_ref.at[pl.ds(0, tm)], buf_ref.at[slot, k], sem.at[slot]).wait()
    acc = h_ref[...]
    cw = cw_ref[...]
    for k in range(TOP_K):
        acc = acc + cw[:, k:k + 1] * buf_ref[slot, k]
    h2_ref[...] = acc
    hn_ref[...] = _rms(acc, g_ref[...]).astype(hn_ref.dtype)


def moe_combine(h, cw, ys, pos_flat, norm_g, tm):
    t, d = h.shape
    grid_spec = pltpu.PrefetchScalarGridSpec(
        num_scalar_prefetch=1,
        grid=(t // tm,),
        in_specs=[pl.BlockSpec((tm, d), lambda i, pos: (i, 0)),
                  pl.BlockSpec((tm, LANE), lambda i, pos: (i, 0)),
                  pl.BlockSpec((1, d), lambda i, pos: (0, 0)),
                  pl.BlockSpec(memory_space=pl.ANY)],
        out_specs=[pl.BlockSpec((tm, d), lambda i, pos: (i, 0)),
                   pl.BlockSpec((tm, d), lambda i, pos: (i, 0))],
        scratch_shapes=[pltpu.VMEM((2, TOP_K, tm, d), F32), pltpu.SemaphoreType.DMA((2,))],
    )
    return pl.pallas_call(
        functools.partial(_combine_body, tm=tm),
        grid_spec=grid_spec,
        out_shape=[jax.ShapeDtypeStruct((t, d), F32), jax.ShapeDtypeStruct((t, d), BF16)],
        compiler_params=_cp(("arbitrary",)),
        name="moe_combine",
    )(pos_flat, h, cw, norm_g.reshape(1, -1).astype(F32), ys)


def _ple_body(a_ref, w_ref, p_ref, wp_ref, h_ref, o_ref, wb_ref):
    @pl.when(pl.program_id(1) == 0)
    def _():
        wb_ref[...] = w_ref[...].astype(BF16)

    gate = _dot(a_ref[...], wb_ref[...])
    pp = _dot(p_ref[...].astype(BF16), wp_ref[...].astype(BF16))
    o_ref[...] = h_ref[...] + jax.nn.sigmoid(gate) * pp


def ple_gate(hn, w_gate, p, w_proj, h, *, tm=1024, tn=512):
    t, d = h.shape
    pd = p.shape[1]
    tm, tn = _tile(t, tm), _tile(d, tn)
    return pl.pallas_call(
        _ple_body,
        grid=(d // tn, t // tm),
        in_specs=[pl.BlockSpec((tm, d), lambda j, i: (i, 0)),
                  pl.BlockSpec((d, tn), lambda j, i: (0, j)),
                  pl.BlockSpec((tm, pd), lambda j, i: (i, 0)),
                  pl.BlockSpec((pd, tn), lambda j, i: (0, j)),
                  pl.BlockSpec((tm, tn), lambda j, i: (i, j))],
        out_specs=pl.BlockSpec((tm, tn), lambda j, i: (i, j)),
        out_shape=jax.ShapeDtypeStruct((t, d), F32),
        scratch_shapes=[pltpu.VMEM((d, tn), BF16)],
        compiler_params=_cp(("parallel", "arbitrary")),
        name="ple_gate",
    )(hn, w_gate, p, w_proj, h)


def _split_w_in(w_in, qkw, vw, rank, qr, kvr, d_model):
    n_gla = 2 * qkw + 2 * vw
    off_gf = n_gla
    off_cq = off_gf + rank
    off_ckv = off_cq + qr
    off_kr = off_ckv + kvr
    off_gates = off_kr + MLA_ROPE
    kr = w_in[:, off_kr:off_gates]
    mid = jnp.concatenate([w_in[:, off_cq:off_kr], _pad_lane(kr), _pad_lane(_swap_halves(kr)),
                           _pad_lane(w_in[:, off_gf:off_cq])], axis=1)
    mid = jnp.pad(mid, ((0, 0), (0, -mid.shape[1] % 512)))
    lay = {"gq": 0, "gk": qkw, "gv": 2 * qkw, "gg": 2 * qkw + vw,
           "cq": 0, "ckv": qr, "krA": qr + kvr, "krB": qr + kvr + LANE, "gf": qr + kvr + 2 * LANE,
           "gate_gla": 0, "gate_mla": d_model}
    return (n_gla, mid.astype(BF16), w_in[:, off_gates:].astype(BF16)), lay


def _layer(h, p, w_in, gla_gate_up, gla_gate_bias, gla_out_norm, mla_q_norm, mla_kv_norm, mla_w_uq,
           mla_w_ukv, w_branch_gla, w_branch_mla, w_out, norm_mix, norm_moe, router_w, router_b,
           expert_w_gate_up, expert_b_gate_up, expert_w_down, expert_b_down, norm_ple, ple_gate_w,
           ple_proj_w, batch, seq):
    t, d = h.shape
    rank, qkw = gla_gate_up.shape
    vw = w_branch_gla.shape[0]
    qr, kvr = mla_q_norm.shape[0], mla_kv_norm.shape[0]
    (n_gla, w_mid, w_gates), lay = _split_w_in(w_in, qkw, vw, rank, qr, kvr, d)

    xn = rms_norm(h, norm_mix)
    proj_gla = matmul_f32_weight(xn, w_in, n_cols=n_gla, out_dtype=BF16, name="in_proj_gla")
    proj_mid = matmul(xn, w_mid, out_dtype=BF16, name="in_proj_mid")
    proj_gates = matmul(xn, w_gates, out_dtype=BF16, name="in_proj_gates")
    o_gla = gla_mixer(proj_gla, proj_mid, lay, gla_gate_up, gla_gate_bias, gla_out_norm, batch, seq)
    o_mla = mla_mixer(proj_mid, lay, mla_q_norm, mla_kv_norm, mla_w_uq, mla_w_ukv, batch, seq)
    merged = merge_branches(o_gla, o_mla, w_branch_gla, w_branch_mla, proj_gates, lay, d)
    h1 = matmul_f32_weight(merged, w_out, res=h, out_dtype=F32, name="out_proj")

    ne = router_w.shape[1]
    tm_e = _tile(t, 256)
    hn, idx, rank_in, cw, cnt = moe_router(h1, norm_moe, router_w, router_b, tm_e)
    counts = cnt[0, :ne].astype(I32)
    tiles_e = -(-counts // tm_e)
    gend = jnp.cumsum(tiles_e) * tm_e
    gstart = gend - tiles_e * tm_e
    pos = (gstart[idx[:, :TOP_K]] + rank_in[:, :TOP_K]).reshape(-1)
    n_tiles = t * TOP_K // tm_e + ne
    tile_start = jnp.arange(n_tiles, dtype=I32) * tm_e
    tile_expert = jnp.minimum(jnp.sum((gend[None, :] <= tile_start[:, None]).astype(I32), axis=1), ne - 1)
    n_used = (gend[-1:] // tm_e).astype(I32)
    sched = _weight_stream_schedule(tile_expert, n_used, tiles_e, gstart // tm_e)
    src = moe_sources(pos, n_tiles * tm_e)
    ys = moe_experts(hn, src, tile_expert, n_used, sched, expert_w_gate_up, expert_b_gate_up,
                     expert_w_down, expert_b_down, tm_e)
    h2, hn2 = moe_combine(h1, cw, ys, pos, norm_ple, _tile(t, 128))

    return ple_gate(hn2, ple_gate_w, p, ple_proj_w, h2)


def kernel(x, p, w_in, gla_gate_up, gla_gate_bias, gla_out_norm, mla_q_norm, mla_kv_norm, mla_w_uq, mla_w_ukv, w_branch_gla, w_branch_mla, w_out, norm_mix, norm_moe, router_w, router_b, expert_w_gate_up, expert_b_gate_up, expert_w_down, expert_b_down, norm_ple, ple_gate_w, ple_proj_w, final_norm):
    batch, seq, d = x.shape
    h = x.reshape(batch * seq, d)
    for i in range(p.shape[0]):
        h = _layer(h, p[i].reshape(batch * seq, -1), w_in[i], gla_gate_up[i], gla_gate_bias[i],
                   gla_out_norm[i], mla_q_norm[i], mla_kv_norm[i], mla_w_uq[i], mla_w_ukv[i],
                   w_branch_gla[i], w_branch_mla[i], w_out[i], norm_mix[i], norm_moe[i], router_w[i],
                   router_b[i], expert_w_gate_up[i], expert_b_gate_up[i], expert_w_down[i],
                   expert_b_down[i], norm_ple[i], ple_gate_w[i], ple_proj_w[i], batch, seq)
    return rms_norm(h, final_norm, out_dtype=F32).reshape(batch, seq, d)
```

```python
import functools
import math

import jax
import jax.numpy as jnp
from jax import lax
from jax.experimental import pallas as pl
from jax.experimental.pallas import tpu as pltpu

F32 = jnp.float32
BF16 = jnp.bfloat16
U32 = jnp.uint32
I32 = jnp.int32

CHUNK = 64
EPS = 1e-6
GLA_HEADS = 8
GLA_TAU = 16.0
MLA_HEADS = 16
MLA_NOPE = 128
MLA_ROPE = 64
MLA_V = 128
ROPE_THETA = 10000.0
TOP_K = 4
SWIGLU_LIMIT = 7.0
SWIGLU_ALPHA = 1.702

LANE = 128
SUBLANE = 8
VMEM_LIMIT = 56 * 2 ** 20
NEG_BIG = -1e30


def _tile(n, pref):
    if n <= pref:
        return n
    t = pref
    while t >= 8:
        if n % t == 0 and t % 8 == 0:
            return t
        t -= 8
    return n


def _cp(sem):
    return pltpu.CompilerParams(dimension_semantics=sem, vmem_limit_bytes=VMEM_LIMIT)


def _dot(a, b):
    return jnp.dot(a, b, preferred_element_type=F32)


def _dot_nt(a, b):
    return lax.dot_general(a, b, (((1,), (1,)), ((), ())), preferred_element_type=F32)


def _dot_tn(a, b):
    return lax.dot_general(a, b, (((0,), (0,)), ((), ())), preferred_element_type=F32)


def _rms(x, g):
    return x * lax.rsqrt(jnp.mean(x * x, axis=-1, keepdims=True) + EPS) * g


def _rms_body(x_ref, g_ref, o_ref):
    o_ref[...] = _rms(x_ref[...].astype(F32), g_ref[...]).astype(o_ref.dtype)


def rms_norm(x, g, *, col_block=0, width=None, out_dtype=BF16, tm=512):
    m = x.shape[0]
    width = width or x.shape[1]
    tm = _tile(m, tm)
    return pl.pallas_call(
        _rms_body,
        grid=(m // tm,),
        in_specs=[pl.BlockSpec((tm, width), lambda i: (i, col_block)),
                  pl.BlockSpec((1, width), lambda i: (0, 0))],
        out_specs=pl.BlockSpec((tm, width), lambda i: (i, 0)),
        out_shape=jax.ShapeDtypeStruct((m, width), out_dtype),
        compiler_params=_cp(("parallel",)),
        name="rms_norm",
    )(x, g.reshape(1, -1).astype(F32))


def _mm_body(a_ref, w_ref, o_ref):
    o_ref[...] = _dot(a_ref[...], w_ref[...]).astype(o_ref.dtype)


def matmul(a, w, *, out_dtype, tm=1024, tn=1024, name="matmul"):
    m, k = a.shape
    n = w.shape[1]
    tm, tn = _tile(m, tm), _tile(n, tn)
    return pl.pallas_call(
        _mm_body,
        grid=(m // tm, n // tn),
        in_specs=[pl.BlockSpec((tm, k), lambda i, j: (i, 0)),
                  pl.BlockSpec((k, tn), lambda i, j: (0, j))],
        out_specs=pl.BlockSpec((tm, tn), lambda i, j: (i, j)),
        out_shape=jax.ShapeDtypeStruct((m, n), out_dtype),
        compiler_params=_cp(("parallel", "arbitrary")),
        name=name,
    )(a, w)


def _norm_mm_body(x_ref, g_ref, w_ref, o_ref, xn_ref):
    @pl.when(pl.program_id(1) == 0)
    def _():
        xn_ref[...] = _rms(x_ref[...], g_ref[...]).astype(xn_ref.dtype)

    o_ref[...] = _dot(xn_ref[...], w_ref[...]).astype(o_ref.dtype)


def norm_matmul(x, g, w, *, tm=512, tn=1024, name):
    m, k = x.shape
    n = w.shape[1]
    tm, tn = _tile(m, tm), _tile(n, tn)
    return pl.pallas_call(
        _norm_mm_body,
        grid=(m // tm, n // tn),
        in_specs=[pl.BlockSpec((tm, k), lambda i, j: (i, 0)),
                  pl.BlockSpec((1, k), lambda i, j: (0, 0)),
                  pl.BlockSpec((k, tn), lambda i, j: (0, j))],
        out_specs=[pl.BlockSpec((tm, tn), lambda i, j: (i, j)),
                   pl.BlockSpec((tm, k), lambda i, j: (i, 0))],
        out_shape=[jax.ShapeDtypeStruct((m, n), BF16), jax.ShapeDtypeStruct((m, k), BF16)],
        compiler_params=_cp(("parallel", "arbitrary")),
        name=name,
    )(x, g.reshape(1, -1).astype(F32), w)


def _gla_body(q_ref, k_ref, v_ref, gg_ref, gf_ref, gup_ref, gb_ref, on_ref, tri_ref, blk_ref,
              o_ref, *scratch, nchunk, scale, hg, dk, dv):
    st_refs, ob_refs = scratch[:hg], scratch[hg:]

    @pl.when(pl.program_id(2) == 0)
    def _():
        for st_ref in st_refs:
            st_ref[...] = jnp.zeros_like(st_ref)

    logit = _dot(gf_ref[...], gup_ref[...]) + gb_ref[...]
    la = (jnp.minimum(logit, 0.0) - jnp.log(1.0 + jnp.exp(-jnp.abs(logit)))) * (1.0 / GLA_TAU)
    la_hi = la.astype(BF16)
    la_lo = (la - la_hi.astype(F32)).astype(BF16)
    tri = tri_ref[...]
    blk = blk_ref[...]
    b = _dot(tri, la_hi) + _dot(tri, la_lo)
    bl = _dot(blk, la_hi) + _dot(blk, la_lo)
    q = q_ref[...].astype(F32)
    k = k_ref[...].astype(F32)
    qd = (q * scale * jnp.exp(b)).astype(BF16)
    kd = (k * jnp.exp(-b)).astype(BF16)
    ke = (k * jnp.exp(bl - b)).astype(BF16)
    dec = jnp.exp(bl)
    row = lax.broadcasted_iota(I32, (CHUNK, CHUNK), 0)
    col = lax.broadcasted_iota(I32, (CHUNK, CHUNK), 1)
    causal = row >= col
    for c in range(nchunk):
        lo = c * CHUNK
        for h in range(hg):
            ks, vs = slice(h * dk, (h + 1) * dk), slice(h * dv, (h + 1) * dv)
            qc = qd[lo:lo + CHUNK, ks]
            kc = kd[lo:lo + CHUNK, ks]
            kec = ke[lo:lo + CHUNK, ks]
            vc = v_ref[lo:lo + CHUNK, vs]
            s = jnp.where(causal, _dot_nt(qc, kc), 0.0).astype(BF16)
            st = st_refs[h][...]
            ob_refs[h][lo:lo + CHUNK, :] = _dot(s, vc) + _dot_nt(qc, st.astype(BF16))
            st_refs[h][...] = st * dec[lo:lo + 1, ks] + _dot_tn(vc, kec)
    for h in range(hg):
        vs = slice(h * dv, (h + 1) * dv)
        o = _rms(ob_refs[h][...], on_ref[...])
        g = gg_ref[:, vs].astype(F32)
        o_ref[:, vs] = (o * (g * jax.nn.sigmoid(g))).astype(o_ref.dtype)


def gla_mixer(proj, proj_mid, lay, gate_up, gate_bias, out_norm, batch, seq):
    heads = GLA_HEADS
    dk = gate_up.shape[1] // heads
    dv = out_norm.shape[0]
    rank = gate_up.shape[0]
    lb = _tile(seq, 512)
    nchunk = lb // CHUNK
    nsb = seq // lb
    t = batch * seq
    gup = jnp.zeros((LANE, heads * dk), F32).at[:rank].set(gate_up).astype(BF16)
    r = jnp.arange(lb)
    same = (r[:, None] // CHUNK) == (r[None, :] // CHUNK)
    tri = (same & (r[None, :] <= r[:, None])).astype(BF16)
    blk = same.astype(BF16)
    hg = 2 if heads % 2 == 0 else 1
    wk, wv = hg * dk, hg * dv
    qb, kb = lay["gq"] // wk, lay["gk"] // wk
    vb, gb = lay["gv"] // wv, lay["gg"] // wv
    fb = lay["gf"] // LANE
    rowmap = lambda b, h, s: b * nsb + s
    return pl.pallas_call(
        functools.partial(_gla_body, nchunk=nchunk, scale=dk ** -0.5, hg=hg, dk=dk, dv=dv),
        grid=(batch, heads // hg, nsb),
        in_specs=[
            pl.BlockSpec((lb, wk), lambda b, h, s: (rowmap(b, h, s), qb + h)),
            pl.BlockSpec((lb, wk), lambda b, h, s: (rowmap(b, h, s), kb + h)),
            pl.BlockSpec((lb, wv), lambda b, h, s: (rowmap(b, h, s), vb + h)),
            pl.BlockSpec((lb, wv), lambda b, h, s: (rowmap(b, h, s), gb + h)),
            pl.BlockSpec((lb, LANE), lambda b, h, s: (rowmap(b, h, s), fb)),
            pl.BlockSpec((LANE, wk), lambda b, h, s: (0, h)),
            pl.BlockSpec((1, wk), lambda b, h, s: (0, h)),
            pl.BlockSpec((1, dv), lambda b, h, s: (0, 0)),
            pl.BlockSpec((lb, lb), lambda b, h, s: (0, 0)),
            pl.BlockSpec((lb, lb), lambda b, h, s: (0, 0)),
        ],
        out_specs=pl.BlockSpec((lb, wv), lambda b, h, s: (rowmap(b, h, s), h)),
        out_shape=jax.ShapeDtypeStruct((t, heads * dv), BF16),
        scratch_shapes=[pltpu.VMEM((dv, dk), F32)] * hg + [pltpu.VMEM((lb, dv), F32)] * hg,
        compiler_params=_cp(("parallel", "parallel", "arbitrary")),
        name="gla_mixer",
    )(proj, proj, proj, proj, proj_mid, gup, gate_bias.reshape(1, -1).astype(F32),
      out_norm.reshape(1, -1).astype(F32), tri, blk)


def _q_up_body(a_ref, w_ref, c_ref, s_ref, o_ref, *, hpt, scale):
    acc = _dot(a_ref[...], w_ref[...])
    c = c_ref[...]
    s = s_ref[...]
    for h in range(hpt):
        src = h * 3 * LANE
        dst = h * 2 * LANE
        o_ref[:, dst:dst + LANE] = (acc[:, src:src + LANE] * scale).astype(o_ref.dtype)
        pe = acc[:, src + LANE:src + 2 * LANE] * c + acc[:, src + 2 * LANE:src + 3 * LANE] * s
        o_ref[:, dst + LANE:dst + 2 * LANE] = (pe * scale).astype(o_ref.dtype)


def _k_up_body(a_ref, w_ref, kra_ref, krb_ref, c_ref, s_ref, o_ref, *, hpt):
    acc = _dot(a_ref[...], w_ref[...])
    kpe = (kra_ref[...].astype(F32) * c_ref[...] + krb_ref[...].astype(F32) * s_ref[...]).astype(o_ref.dtype)
    for h in range(hpt):
        o_ref[:, 2 * h * LANE:(2 * h + 1) * LANE] = acc[:, h * LANE:(h + 1) * LANE].astype(o_ref.dtype)
        o_ref[:, (2 * h + 1) * LANE:(2 * h + 2) * LANE] = kpe


def _v_up_body(a_ref, w_ref, o_ref, *, hpt):
    acc = _dot(a_ref[...], w_ref[...])
    lane = lax.broadcasted_iota(I32, (acc.shape[0], LANE), 1)
    ones_col = (lane == 0).astype(o_ref.dtype)
    for h in range(hpt):
        o_ref[:, 2 * h * LANE:(2 * h + 1) * LANE] = acc[:, h * LANE:(h + 1) * LANE].astype(o_ref.dtype)
        o_ref[:, (2 * h + 1) * LANE:(2 * h + 2) * LANE] = ones_col


def _swap_halves(w):
    half = w.shape[-1] // 2
    return jnp.concatenate([w[..., half:], w[..., :half]], axis=-1)


def _pad_lane(w):
    pad = LANE - w.shape[-1]
    return jnp.pad(w, [(0, 0)] * (w.ndim - 1) + [(0, pad)])


def mla_mixer(proj, lay, q_norm, kv_norm, w_uq, w_ukv, batch, seq):
    heads = MLA_HEADS
    t = batch * seq
    qr, kvr = q_norm.shape[0], kv_norm.shape[0]
    assert MLA_NOPE == LANE and MLA_V == LANE and MLA_ROPE <= LANE
    scale = (MLA_NOPE + MLA_ROPE) ** -0.5 * math.log2(math.e)
    tm = _tile(seq, 1024)
    nrb = seq // tm
    hpt = 4 if heads % 4 == 0 else 1

    pos = jnp.arange(seq, dtype=F32)
    inv_freq = ROPE_THETA ** (-jnp.arange(0, MLA_ROPE, 2, dtype=F32) / MLA_ROPE)
    ang = pos[:, None] * inv_freq[None, :]
    cos, sin = jnp.cos(ang), jnp.sin(ang)
    ctab = _pad_lane(jnp.concatenate([cos, cos], axis=-1))
    stab = _pad_lane(jnp.concatenate([-sin, sin], axis=-1))

    cqn = rms_norm(proj, q_norm, col_block=lay["cq"] // qr, width=qr)
    ckvn = rms_norm(proj, kv_norm, col_block=lay["ckv"] // kvr, width=kvr)

    wq = w_uq.reshape(qr, heads, MLA_NOPE + MLA_ROPE)
    wq_pe = wq[..., MLA_NOPE:]
    wq_r = jnp.concatenate([wq[..., :MLA_NOPE], _pad_lane(wq_pe), _pad_lane(_swap_halves(wq_pe))],
                           axis=-1).reshape(qr, heads * 3 * LANE).astype(BF16)
    q_cat = pl.pallas_call(
        functools.partial(_q_up_body, hpt=hpt, scale=scale),
        grid=(t // tm, heads // hpt),
        in_specs=[pl.BlockSpec((tm, qr), lambda i, j: (i, 0)),
                  pl.BlockSpec((qr, hpt * 3 * LANE), lambda i, j: (0, j)),
                  pl.BlockSpec((tm, LANE), lambda i, j: (i % nrb, 0)),
                  pl.BlockSpec((tm, LANE), lambda i, j: (i % nrb, 0))],
        out_specs=pl.BlockSpec((tm, hpt * 2 * LANE), lambda i, j: (i, j)),
        out_shape=jax.ShapeDtypeStruct((t, heads * 2 * LANE), BF16),
        compiler_params=_cp(("parallel", "arbitrary")),
        name="mla_q_up",
    )(cqn, wq_r, ctab, stab)

    wkv = w_ukv.reshape(kvr, heads, MLA_NOPE + MLA_V)
    wk = wkv[..., :MLA_NOPE].reshape(kvr, heads * MLA_NOPE).astype(BF16)
    wv = wkv[..., MLA_NOPE:].reshape(kvr, heads * MLA_V).astype(BF16)
    ka, kb = lay["krA"] // LANE, lay["krB"] // LANE
    k_cat = pl.pallas_call(
        functools.partial(_k_up_body, hpt=hpt),
        grid=(t // tm, heads // hpt),
        in_specs=[pl.BlockSpec((tm, kvr), lambda i, j: (i, 0)),
                  pl.BlockSpec((kvr, hpt * LANE), lambda i, j: (0, j)),
                  pl.BlockSpec((tm, LANE), lambda i, j: (i, ka)),
                  pl.BlockSpec((tm, LANE), lambda i, j: (i, kb)),
                  pl.BlockSpec((tm, LANE), lambda i, j: (i % nrb, 0)),
                  pl.BlockSpec((tm, LANE), lambda i, j: (i % nrb, 0))],
        out_specs=pl.BlockSpec((tm, hpt * 2 * LANE), lambda i, j: (i, j)),
        out_shape=jax.ShapeDtypeStruct((t, heads * 2 * LANE), BF16),
        compiler_params=_cp(("parallel", "arbitrary")),
        name="mla_k_up",
    )(ckvn, wk, proj, proj, ctab, stab)
    v_ext = pl.pallas_call(
        functools.partial(_v_up_body, hpt=hpt),
        grid=(t // tm, heads // hpt),
        in_specs=[pl.BlockSpec((tm, kvr), lambda i, j: (i, 0)),
                  pl.BlockSpec((kvr, hpt * LANE), lambda i, j: (0, j))],
        out_specs=pl.BlockSpec((tm, hpt * 2 * LANE), lambda i, j: (i, j)),
        out_shape=jax.ShapeDtypeStruct((t, heads * 2 * LANE), BF16),
        compiler_params=_cp(("parallel", "arbitrary")),
        name="mla_v_up",
    )(ckvn, wv)
    return mla_attention(q_cat, k_cat, v_ext, batch, seq)


def _mla_attn_body(q_ref, k_ref, v_ref, o_ref, m_ref, acc_ref, *, tq, tk, hg):
    qi = pl.program_id(2)
    m_ref[...] = jnp.full_like(m_ref, NEG_BIG)
    acc_ref[...] = jnp.zeros_like(acc_ref)
    w = 2 * LANE

    def block(ki, masked):
        ks = pl.multiple_of(ki * tk, tk)
        for g in range(hg):
            s = _dot_nt(q_ref[:, g * w:(g + 1) * w], k_ref[pl.ds(ks, tk), g * w:(g + 1) * w])
            if masked:
                qc = (qi * tq + lax.broadcasted_iota(I32, (tq, tk), 0)) // CHUNK
                kc = (ki * tk + lax.broadcasted_iota(I32, (tq, tk), 1)) // CHUNK
                s = jnp.where(kc <= qc, s, NEG_BIG)
            m_prev = m_ref[g]
            m_new = jnp.maximum(m_prev, jnp.max(s, axis=-1, keepdims=True))
            p = jnp.exp2(s - m_new).astype(BF16)
            acc_ref[g] = (jnp.exp2(m_prev - m_new) * acc_ref[g]
                          + _dot(p, v_ref[pl.ds(ks, tk), g * w:(g + 1) * w]))
            m_ref[g] = m_new

    def body(ki, carry):
        block(ki, False)
        return carry

    n_full = qi * (tq // tk)
    lax.fori_loop(0, n_full, body, 0)
    for dblk in range(tq // tk):
        block(n_full + dblk, True)
    for g in range(hg):
        acc = acc_ref[g]
        o_ref[:, g * LANE:(g + 1) * LANE] = (acc[:, :LANE] / acc[:, LANE:LANE + 1]).astype(o_ref.dtype)


def mla_attention(q_cat, k_cat, v_ext, batch, seq):
    heads = MLA_HEADS
    t = batch * seq
    tq = _tile(seq, 512)
    tk = tq
    nq = seq // tq
    hg = 4 if heads % 4 == 0 else 1
    w = 2 * LANE
    return pl.pallas_call(
        functools.partial(_mla_attn_body, tq=tq, tk=tk, hg=hg),
        grid=(batch, heads // hg, nq),
        in_specs=[pl.BlockSpec((tq, hg * w), lambda b, h, i: (b * nq + i, h)),
                  pl.BlockSpec((seq, hg * w), lambda b, h, i: (b, h)),
                  pl.BlockSpec((seq, hg * w), lambda b, h, i: (b, h))],
        out_specs=pl.BlockSpec((tq, hg * LANE), lambda b, h, i: (b * nq + i, h)),
        out_shape=jax.ShapeDtypeStruct((t, heads * MLA_V), BF16),
        scratch_shapes=[pltpu.VMEM((hg, tq, 1), F32), pltpu.VMEM((hg, tq, w), F32)],
        compiler_params=_cp(("parallel", "parallel", "arbitrary")),
        name="mla_attention",
    )(q_cat, k_cat, v_ext)


def _merge_body(a1_ref, w1_ref, a2_ref, w2_ref, g1_ref, g2_ref, o_ref):
    y1 = _dot(a1_ref[...], w1_ref[...])
    y2 = _dot(a2_ref[...], w2_ref[...])
    o_ref[...] = (jax.nn.sigmoid(g1_ref[...].astype(F32)) * y1
                  + jax.nn.sigmoid(g2_ref[...].astype(F32)) * y2).astype(o_ref.dtype)


def merge_branches(o_gla, o_mla, w_gla, w_mla, proj, lay, d_model):
    t = o_gla.shape[0]
    tm = _tile(t, 1024)
    tn = _tile(d_model, 512)
    g1, g2 = lay["gate_gla"] // tn, lay["gate_mla"] // tn
    k1, k2 = o_gla.shape[1], o_mla.shape[1]
    return pl.pallas_call(
        _merge_body,
        grid=(t // tm, d_model // tn),
        in_specs=[pl.BlockSpec((tm, k1), lambda i, j: (i, 0)),
                  pl.BlockSpec((k1, tn), lambda i, j: (0, j)),
                  pl.BlockSpec((tm, k2), lambda i, j: (i, 0)),
                  pl.BlockSpec((k2, tn), lambda i, j: (0, j)),
                  pl.BlockSpec((tm, tn), lambda i, j: (i, g1 + j)),
                  pl.BlockSpec((tm, tn), lambda i, j: (i, g2 + j))],
        out_specs=pl.BlockSpec((tm, tn), lambda i, j: (i, j)),
        out_shape=jax.ShapeDtypeStruct((t, d_model), BF16),
        compiler_params=_cp(("parallel", "arbitrary")),
        name="merge_branches",
    )(o_gla, w_gla, o_mla, w_mla, proj, proj)


def _mm_res_body(a_ref, w_ref, r_ref, o_ref):
    o_ref[...] = r_ref[...] + _dot(a_ref[...], w_ref[...])


def matmul_residual(a, w, res, *, tm=1024, tn=512, name="matmul_residual"):
    m, k = a.shape
    n = w.shape[1]
    tm, tn = _tile(m, tm), _tile(n, tn)
    return pl.pallas_call(
        _mm_res_body,
        grid=(m // tm, n // tn),
        in_specs=[pl.BlockSpec((tm, k), lambda i, j: (i, 0)),
                  pl.BlockSpec((k, tn), lambda i, j: (0, j)),
                  pl.BlockSpec((tm, tn), lambda i, j: (i, j))],
        out_specs=pl.BlockSpec((tm, tn), lambda i, j: (i, j)),
        out_shape=jax.ShapeDtypeStruct((m, n), F32),
        compiler_params=_cp(("parallel", "arbitrary")),
        name=name,
    )(a, w, res)


def _router_body(h_ref, g_ref, rwh_ref, rwl_ref, rb_ref, tri_ref, hn_ref, idx_ref, rank_ref, cw_ref, cnt_ref,
                 carry_ref):
    @pl.when(pl.program_id(0) == 0)
    def _():
        carry_ref[...] = jnp.zeros_like(carry_ref)

    hn = _rms(h_ref[...], g_ref[...])
    hn_ref[...] = hn
    hn_hi = hn.astype(BF16)
    hn_lo = (hn - hn_hi.astype(F32)).astype(BF16)
    logits = (_dot(hn_hi, rwh_ref[...]) + _dot(hn_hi, rwl_ref[...]) + _dot(hn_lo, rwh_ref[...])) + rb_ref[...]
    tm = logits.shape[0]
    lane = lax.broadcasted_iota(I32, (tm, LANE), 1)
    work = logits
    vals, sels, idxs = [], [], []
    for _ in range(TOP_K):
        mx = jnp.max(work, axis=-1, keepdims=True)
        ix = jnp.min(jnp.where(work == mx, lane, LANE), axis=-1, keepdims=True)
        sel = lane == ix
        work = jnp.where(sel, -jnp.inf, work)
        vals.append(mx)
        sels.append(sel)
        idxs.append(ix)
    es = [jnp.exp(v - vals[0]) for v in vals]
    den = es[0]
    for e in es[1:]:
        den = den + e
    hot = jnp.zeros((tm, LANE), F32)
    for sel in sels:
        hot = hot + sel.astype(F32)
    carry = carry_ref[0:1, :]
    rank = _dot(tri_ref[...], hot.astype(BF16)) + carry
    idx_out = jnp.zeros((tm, LANE), I32)
    rank_out = jnp.zeros((tm, LANE), F32)
    cw_out = jnp.zeros((tm, LANE), F32)
    for k in range(TOP_K):
        rk = jnp.sum(jnp.where(sels[k], rank, 0.0), axis=-1, keepdims=True)
        idx_out = jnp.where(lane == k, idxs[k], idx_out)
        rank_out = jnp.where(lane == k, rk, rank_out)
        cw_out = jnp.where(lane == k, es[k] / den, cw_out)
    idx_ref[...] = idx_out
    rank_ref[...] = rank_out.astype(I32)
    cw_ref[...] = cw_out
    new_carry = carry + jnp.sum(hot, axis=0, keepdims=True)
    carry_ref[...] = jnp.broadcast_to(new_carry, carry_ref.shape)
    cnt_ref[...] = jnp.broadcast_to(new_carry, cnt_ref.shape)


def moe_router(h, norm_g, router_w, router_b, tm):
    t, d = h.shape
    ne = router_w.shape[1]
    assert ne <= LANE and d % (2 * LANE) == 0
    rw = jnp.zeros((d, LANE), F32).at[:, :ne].set(router_w)
    rw_hi = rw.astype(BF16)
    rw_lo = (rw - rw_hi.astype(F32)).astype(BF16)
    rb = jnp.full((1, LANE), NEG_BIG, F32).at[0, :ne].set(router_b)
    r = jnp.arange(tm)
    tri = (r[None, :] < r[:, None]).astype(BF16)
    row = lambda i: (i, 0)
    const = lambda i: (0, 0)
    return pl.pallas_call(
        _router_body,
        grid=(t // tm,),
        in_specs=[pl.BlockSpec((tm, d), row), pl.BlockSpec((1, d), const),
                  pl.BlockSpec((d, LANE), const), pl.BlockSpec((d, LANE), const),
                  pl.BlockSpec((1, LANE), const), pl.BlockSpec((tm, tm), const)],
        out_specs=[pl.BlockSpec((tm, d), row), pl.BlockSpec((tm, LANE), row),
                   pl.BlockSpec((tm, LANE), row), pl.BlockSpec((tm, LANE), row),
                   pl.BlockSpec((8, LANE), const)],
        out_shape=[jax.ShapeDtypeStruct((t, d), F32), jax.ShapeDtypeStruct((t, LANE), I32),
                   jax.ShapeDtypeStruct((t, LANE), I32), jax.ShapeDtypeStruct((t, LANE), F32),
                   jax.ShapeDtypeStruct((8, LANE), F32)],
        scratch_shapes=[pltpu.VMEM((8, LANE), F32)],
        compiler_params=_cp(("arbitrary",)),
        name="moe_router",
    )(h, norm_g.reshape(1, -1).astype(F32), rw_hi, rw_lo, rb, tri)


def _sources_body(pos_ref, src_ref, *, nsteps):
    i = pl.program_id(0)
    rows_per = src_ref.shape[0] // nsteps
    toks_per = pos_ref.shape[0] // TOP_K // nsteps

    @pl.when(i < nsteps)
    def _():
        def clear(r, carry):
            src_ref[i * rows_per + r] = 0
            return carry

        lax.fori_loop(0, rows_per, clear, 0, unroll=8)

    @pl.when(i >= nsteps)
    def _():
        t0 = (i - nsteps) * toks_per

        def put(j, carry):
            for k in range(TOP_K):
                src_ref[pos_ref[(t0 + j) * TOP_K + k]] = t0 + j
            return carry

        lax.fori_loop(0, toks_per, put, 0, unroll=2)


def moe_sources(pos_flat, n_rows):
    n_tok = pos_flat.shape[0] // TOP_K
    nsteps = 64 if (n_rows % 64 == 0 and n_tok % 64 == 0) else 1
    return pl.pallas_call(
        functools.partial(_sources_body, nsteps=nsteps),
        grid=(2 * nsteps,),
        in_specs=[pl.BlockSpec(memory_space=pltpu.SMEM)],
        out_specs=pl.BlockSpec(memory_space=pltpu.SMEM),
        out_shape=jax.ShapeDtypeStruct((n_rows,), I32),
        compiler_params=_cp(("arbitrary",)),
        name="moe_sources",
    )(pos_flat)


W_CHUNKS = 16


def _moe_up_body(te_ref, nu_ref, clo_ref, chi_ref, nxt_ref, slot_ref, src_ref, hn_ref, w_ref, b_ref, o_ref,
                 xf_ref, xb_ref, wb_ref, st_ref, gsem, wsem, *, tm, de):
    i = pl.program_id(0)
    ch = w_ref.shape[1] // W_CHUNKS

    def w_copy(e, c, buf):
        return pltpu.make_async_copy(w_ref.at[e, pl.ds(c * ch, ch)], st_ref.at[buf], wsem.at[buf])

    def convert(c, buf, slot):
        wb_ref[slot, pl.ds(pl.multiple_of(c * ch, ch), ch), :] = st_ref[buf].astype(BF16)

    def gather(tile):
        base = tile * tm

        def rows8(g, carry):
            r0 = pl.multiple_of(g * SUBLANE, SUBLANE)
            for j in range(SUBLANE):
                pltpu.make_async_copy(hn_ref.at[pl.ds(src_ref[base + r0 + j], 1)],
                                      xf_ref.at[pl.ds(r0 + j, 1)], gsem).start()
            return carry

        lax.fori_loop(0, tm // SUBLANE, rows8, 0)

    @pl.when(i == 0)
    def _():
        e0, s0 = te_ref[0], slot_ref[0]
        w_copy(e0, 0, 0).start()

        def load(c, carry):
            @pl.when(c + 1 < W_CHUNKS)
            def _():
                w_copy(e0, c + 1, (c + 1) % 2).start()

            w_copy(e0, c, c % 2).wait()
            convert(c, c % 2, s0)
            return carry

        lax.fori_loop(0, W_CHUNKS, load, 0)
        gather(0)

    @pl.when(i < nu_ref[0])
    def _():
        pltpu.make_async_copy(hn_ref.at[pl.ds(0, tm)], xf_ref, gsem).wait()
        xb_ref[...] = xf_ref[...].astype(BF16)

        @pl.when(i + 1 < nu_ref[0])
        def _():
            gather(i + 1)

        e_n, s_n = nxt_ref[i], 1 - slot_ref[i]
        lo, hi = clo_ref[i], chi_ref[i]

        @pl.when((lo == 0) & (hi > 0))
        def _():
            w_copy(e_n, 0, 0).start()
            w_copy(e_n, 1, 1).start()

        def chunk(c, carry):
            w_copy(e_n, c, c % 2).wait()
            convert(c, c % 2, s_n)

            @pl.when(c + 2 < W_CHUNKS)
            def _():
                w_copy(e_n, c + 2, c % 2).start()

            return carry

        lax.fori_loop(lo, hi, chunk, 0)

        gu = _dot(xb_ref[...], wb_ref[slot_ref[i]]) + b_ref[...]
        gate = jnp.minimum(gu[:, :de], SWIGLU_LIMIT)
        up = jnp.clip(gu[:, de:], -SWIGLU_LIMIT, SWIGLU_LIMIT)
        o_ref[...] = ((up + 1.0) * (gate * jax.nn.sigmoid(SWIGLU_ALPHA * gate))).astype(o_ref.dtype)

    @pl.when(i >= nu_ref[0])
    def _():
        o_ref[...] = jnp.zeros_like(o_ref)


def _moe_down_body(te_ref, nu_ref, a_ref, w_ref, b_ref, o_ref, wb_ref):
    i = pl.program_id(0)

    @pl.when(i < nu_ref[0])
    def _():
        @pl.when((i == 0) | (te_ref[i] != te_ref[jnp.maximum(i - 1, 0)]))
        def _():
            wb_ref[...] = w_ref[...].astype(BF16)

        o_ref[...] = _dot(a_ref[...], wb_ref[...]) + b_ref[...]

    @pl.when(i >= nu_ref[0])
    def _():
        o_ref[...] = jnp.zeros_like(o_ref)


def _weight_stream_schedule(tile_expert, n_used, tiles_per_expert, group_start_tile):
    ne = tiles_per_expert.shape[0]
    n_tiles = tile_expert.shape[0]
    nonempty = tiles_per_expert > 0
    ordinal = jnp.cumsum(nonempty.astype(I32)) - 1
    cand = jnp.where(nonempty, jnp.arange(ne, dtype=I32), ne)
    next_ge = lax.cummin(cand, axis=0, reverse=True)
    next_gt = jnp.concatenate([next_ge[1:], jnp.full((1,), ne, I32)])
    tile = jnp.arange(n_tiles, dtype=I32)
    k_in = tile - group_start_tile[tile_expert]
    n_in = jnp.maximum(tiles_per_expert[tile_expert], 1)
    nxt = next_gt[tile_expert]
    live = (nxt < ne) & (tile < n_used[0])
    c_lo = jnp.where(live, k_in * W_CHUNKS // n_in, 0).astype(I32)
    c_hi = jnp.where(live, (k_in + 1) * W_CHUNKS // n_in, 0).astype(I32)
    return c_lo, c_hi, jnp.where(live, nxt, 0).astype(I32), (ordinal[tile_expert] % 2).astype(I32)


def moe_experts(hn, src, tile_expert, n_used, sched, w_gu, b_gu, w_dn, b_dn, tm):
    d = hn.shape[1]
    n_rows = src.shape[0]
    ne, _, de2 = w_gu.shape
    de = de2 // 2
    n_tiles = n_rows // tm
    assert d % W_CHUNKS == 0 and W_CHUNKS >= 2
    c_lo, c_hi, nxt, slot = sched
    act = pl.pallas_call(
        functools.partial(_moe_up_body, tm=tm, de=de),
        grid_spec=pltpu.PrefetchScalarGridSpec(
            num_scalar_prefetch=7,
            grid=(n_tiles,),
            in_specs=[pl.BlockSpec(memory_space=pl.ANY),
                      pl.BlockSpec(memory_space=pl.ANY),
                      pl.BlockSpec((None, 1, de2), lambda i, te, *_: (te[i], 0, 0))],
            out_specs=pl.BlockSpec((tm, de), lambda i, *_: (i, 0)),
            scratch_shapes=[pltpu.VMEM((tm, d), F32), pltpu.VMEM((tm, d), BF16),
                            pltpu.VMEM((2, d, de2), BF16), pltpu.VMEM((2, d // W_CHUNKS, de2), F32),
                            pltpu.SemaphoreType.DMA(()), pltpu.SemaphoreType.DMA((2,))],
        ),
        out_shape=jax.ShapeDtypeStruct((n_rows, de), BF16),
        compiler_params=_cp(("arbitrary",)),
        name="moe_up",
    )(tile_expert, n_used, c_lo, c_hi, nxt, slot, src, hn, w_gu, b_gu.reshape(ne, 1, de2))
    return pl.pallas_call(
        _moe_down_body,
        grid_spec=pltpu.PrefetchScalarGridSpec(
            num_scalar_prefetch=2,
            grid=(n_tiles,),
            in_specs=[pl.BlockSpec((tm, de), lambda i, te, nu: (i, 0)),
                      pl.BlockSpec((None, de, d), lambda i, te, nu: (te[i], 0, 0)),
                      pl.BlockSpec((None, 1, d), lambda i, te, nu: (te[i], 0, 0))],
            out_specs=pl.BlockSpec((tm, d), lambda i, te, nu: (i, 0)),
            scratch_shapes=[pltpu.VMEM((de, d), BF16)],
        ),
        out_shape=jax.ShapeDtypeStruct((n_rows, d), F32),
        compiler_params=_cp(("arbitrary",)),
        name="moe_down",
    )(tile_expert, n_used, act, w_dn, b_dn.reshape(ne, 1, d))


def _combine_body(pos_ref, h_ref, cw_ref, g_ref, ys_ref, h2_ref, hn_ref, buf_ref, sem, *, tm):
    i = pl.program_id(0)

    def issue(tile, slot):
        base = tile * (tm * TOP_K)

        def row(r, carry):
            for k in range(TOP_K):
                p = pos_ref[base + r * TOP_K + k]
                pltpu.make_async_copy(ys_ref.at[pl.ds(p, 1)], buf_ref.at[slot, k, pl.ds(r, 1)],
                                      sem.at[slot]).start()
            return carry

        lax.fori_loop(0, tm, row, 0, unroll=2)

    @pl.when(i == 0)
    def _():
        issue(0, 0)

    @pl.when(i + 1 < pl.num_programs(0))
    def _():
        issue(i + 1, (i + 1) % 2)

    slot = i % 2
    for k in range(TOP_K):
        pltpu.make_async_copy(ys_ref.at[pl.ds(0, tm)], buf_ref.at[slot, k], sem.at[slot]).wait()
    acc = h_ref[...]
    cw = cw_ref[...]
    for k in range(TOP_K):
        acc = acc + cw[:, k:k + 1] * buf_ref[slot, k]
    h2_ref[...] = acc
    hn_ref[...] = _rms(acc, g_ref[...]).astype(hn_ref.dtype)


def moe_combine(h, cw, ys, pos_flat, norm_g, tm):
    t, d = h.shape
    grid_spec = pltpu.PrefetchScalarGridSpec(
        num_scalar_prefetch=1,
        grid=(t // tm,),
        in_specs=[pl.BlockSpec((tm, d), lambda i, pos: (i, 0)),
                  pl.BlockSpec((tm, LANE), lambda i, pos: (i, 0)),
                  pl.BlockSpec((1, d), lambda i, pos: (0, 0)),
                  pl.BlockSpec(memory_space=pl.ANY)],
        out_specs=[pl.BlockSpec((tm, d), lambda i, pos: (i, 0)),
                   pl.BlockSpec((tm, d), lambda i, pos: (i, 0))],
        scratch_shapes=[pltpu.VMEM((2, TOP_K, tm, d), F32), pltpu.SemaphoreType.DMA((2,))],
    )
    return pl.pallas_call(
        functools.partial(_combine_body, tm=tm),
        grid_spec=grid_spec,
        out_shape=[jax.ShapeDtypeStruct((t, d), F32), jax.ShapeDtypeStruct((t, d), BF16)],
        compiler_params=_cp(("arbitrary",)),
        name="moe_combine",
    )(pos_flat, h, cw, norm_g.reshape(1, -1).astype(F32), ys)


def _ple_body(a_ref, w_ref, p_ref, wp_ref, h_ref, o_ref):
    gate = _dot(a_ref[...], w_ref[...])
    pp = _dot(p_ref[...].astype(BF16), wp_ref[...])
    o_ref[...] = h_ref[...] + jax.nn.sigmoid(gate) * pp


def ple_gate(hn, w_gate, p, w_proj, h, *, tm=1024, tn=512):
    t, d = h.shape
    pd = p.shape[1]
    tm, tn = _tile(t, tm), _tile(d, tn)
    return pl.pallas_call(
        _ple_body,
        grid=(t // tm, d // tn),
        in_specs=[pl.BlockSpec((tm, d), lambda i, j: (i, 0)),
                  pl.BlockSpec((d, tn), lambda i, j: (0, j)),
                  pl.BlockSpec((tm, pd), lambda i, j: (i, 0)),
                  pl.BlockSpec((pd, tn), lambda i, j: (0, j)),
                  pl.BlockSpec((tm, tn), lambda i, j: (i, j))],
        out_specs=pl.BlockSpec((tm, tn), lambda i, j: (i, j)),
        out_shape=jax.ShapeDtypeStruct((t, d), F32),
        compiler_params=_cp(("parallel", "arbitrary")),
        name="ple_gate",
    )(hn, w_gate, p, w_proj, h)


def _split_w_in(w_in, qkw, vw, rank, qr, kvr, d_model):
    n_gla = 2 * qkw + 2 * vw
    off_gf = n_gla
    off_cq = off_gf + rank
    off_ckv = off_cq + qr
    off_kr = off_ckv + kvr
    off_gates = off_kr + MLA_ROPE
    kr = w_in[:, off_kr:off_gates]
    mid = jnp.concatenate([w_in[:, off_cq:off_kr], _pad_lane(kr), _pad_lane(_swap_halves(kr)),
                           _pad_lane(w_in[:, off_gf:off_cq])], axis=1)
    mid = jnp.pad(mid, ((0, 0), (0, -mid.shape[1] % 512)))
    lay = {"gq": 0, "gk": qkw, "gv": 2 * qkw, "gg": 2 * qkw + vw,
           "cq": 0, "ckv": qr, "krA": qr + kvr, "krB": qr + kvr + LANE, "gf": qr + kvr + 2 * LANE,
           "gate_gla": 0, "gate_mla": d_model}
    return (w_in[:, :n_gla].astype(BF16), mid.astype(BF16), w_in[:, off_gates:].astype(BF16)), lay


def _layer(h, p, w_in, gla_gate_up, gla_gate_bias, gla_out_norm, mla_q_norm, mla_kv_norm, mla_w_uq,
           mla_w_ukv, w_branch_gla, w_branch_mla, w_out, norm_mix, norm_moe, router_w, router_b,
           expert_w_gate_up, expert_b_gate_up, expert_w_down, expert_b_down, norm_ple, ple_gate_w,
           ple_proj_w, batch, seq):
    t, d = h.shape
    rank, qkw = gla_gate_up.shape
    vw = w_branch_gla.shape[0]
    qr, kvr = mla_q_norm.shape[0], mla_kv_norm.shape[0]
    (w_gla, w_mid, w_gates), lay = _split_w_in(w_in, qkw, vw, rank, qr, kvr, d)

    proj_mid, xn = norm_matmul(h, norm_mix, w_mid, name="in_proj_mid")
    proj_gla = matmul(xn, w_gla, out_dtype=BF16, name="in_proj_gla")
    proj_gates = matmul(xn, w_gates, out_dtype=BF16, name="in_proj_gates")
    o_gla = gla_mixer(proj_gla, proj_mid, lay, gla_gate_up, gla_gate_bias, gla_out_norm, batch, seq)
    o_mla = mla_mixer(proj_mid, lay, mla_q_norm, mla_kv_norm, mla_w_uq, mla_w_ukv, batch, seq)
    merged = merge_branches(o_gla, o_mla, w_branch_gla.astype(BF16), w_branch_mla.astype(BF16), proj_gates,
                            lay, d)
    h1 = matmul_residual(merged, w_out.astype(BF16), h, name="out_proj")

    ne = router_w.shape[1]
    tm_e = _tile(t, 256)
    hn, idx, rank_in, cw, cnt = moe_router(h1, norm_moe, router_w, router_b, tm_e)
    counts = cnt[0, :ne].astype(I32)
    tiles_e = -(-counts // tm_e)
    gend = jnp.cumsum(tiles_e) * tm_e
    gstart = gend - tiles_e * tm_e
    pos = (gstart[idx[:, :TOP_K]] + rank_in[:, :TOP_K]).reshape(-1)
    n_tiles = t * TOP_K // tm_e + ne
    tile_start = jnp.arange(n_tiles, dtype=I32) * tm_e
    tile_expert = jnp.minimum(jnp.sum((gend[None, :] <= tile_start[:, None]).astype(I32), axis=1), ne - 1)
    n_used = (gend[-1:] // tm_e).astype(I32)
    sched = _weight_stream_schedule(tile_expert, n_used, tiles_e, gstart // tm_e)
    src = moe_sources(pos, n_tiles * tm_e)
    ys = moe_experts(hn, src, tile_expert, n_used, sched, expert_w_gate_up, expert_b_gate_up,
                     expert_w_down, expert_b_down, tm_e)
    h2, hn2 = moe_combine(h1, cw, ys, pos, norm_ple, _tile(t, 128))

    return ple_gate(hn2, ple_gate_w.astype(BF16), p, ple_proj_w.astype(BF16), h2)


def kernel(x, p, w_in, gla_gate_up, gla_gate_bias, gla_out_norm, mla_q_norm, mla_kv_norm, mla_w_uq, mla_w_ukv, w_branch_gla, w_branch_mla, w_out, norm_mix, norm_moe, router_w, router_b, expert_w_gate_up, expert_b_gate_up, expert_w_down, expert_b_down, norm_ple, ple_gate_w, ple_proj_w, final_norm):
    batch, seq, d = x.shape
    h = x.reshape(batch * seq, d)
    for i in range(p.shape[0]):
        h = _layer(h, p[i].reshape(batch * seq, -1), w_in[i], gla_gate_up[i], gla_gate_bias[i],
                   gla_out_norm[i], mla_q_norm[i], mla_kv_norm[i], mla_w_uq[i], mla_w_ukv[i],
                   w_branch_gla[i], w_branch_mla[i], w_out[i], norm_mix[i], norm_moe[i], router_w[i],
                   router_b[i], expert_w_gate_up[i], expert_b_gate_up[i], expert_w_down[i],
                   expert_b_down[i], norm_ple[i], ple_gate_w[i], ple_proj_w[i], batch, seq)
    return rms_norm(h, final_norm, out_dtype=F32).reshape(batch, seq, d)
```
